```python
import math
import jax, jax.numpy as jnp
from jax import lax
import numpy as np

D_MODEL = 1024
BATCH = 2
SEQ = 8192
DEPTH = 1

CHUNK = 64
D_MIX = D_MODEL
D_POOL = D_MIX // 2
D_LRU = D_MIX - D_POOL
POOL_WINDOWS = (2, 4, 8, 16)
POOL_GROUPS = len(POOL_WINDOWS)
POOL_GW = D_POOL // POOL_GROUPS
LRU_HEADS = 8
LRU_HD = D_LRU // LRU_HEADS
CONV_W = 4
LRU_C = 8.0
PEER_HEADS = 8
PEER_NKEYS = 128
PEER_N = PEER_NKEYS * PEER_NKEYS
PEER_DKEY = 256
PEER_TOPK = 16
PEER_BLOCK = 128
LN_EPS = 1e-5
ALPHA = (2.0 * DEPTH) ** 0.25
BETA = (8.0 * DEPTH) ** -0.25

kernel_name = "hybrid_pool_rglru_peer_deepnorm_adaln"


def layer_norm(x, g, b):
    xf = x.astype(jnp.float32)
    mu = jnp.mean(xf, axis=-1, keepdims=True)
    var = jnp.mean(jnp.square(xf - mu), axis=-1, keepdims=True)
    y = (xf - mu) * lax.rsqrt(var + LN_EPS)
    return (y * g.astype(jnp.float32) + b.astype(jnp.float32)).astype(x.dtype)


def pool_mixer(p, w_grp, scale):
    B, S, _ = p.shape
    pf = p.astype(jnp.float32).reshape(B, S, POOL_GROUPS, POOL_GW)
    cs = jnp.cumsum(pf, axis=1)
    count = jnp.arange(1, S + 1, dtype=jnp.float32)
    outs = []
    for g, w in enumerate(POOL_WINDOWS):
        cg = cs[:, :, g]
        lag = jnp.pad(cg[:, :S - w], ((0, 0), (w, 0), (0, 0)))
        mean = (cg - lag) / jnp.minimum(count, float(w))[None, :, None]
        outs.append(mean - pf[:, :, g])
    y = jnp.stack(outs, axis=2).astype(p.dtype)
    y = jnp.einsum('bsgi,gij->bsgj', y, w_grp).reshape(B, S, D_POOL)
    return y * scale


def _lin_rec_combine(left, right):
    a1, b1 = left
    a2, b2 = right
    return a1 * a2, a2 * b1 + b2


def rglru_mixer(xr, yg, conv_w, conv_b, wa, ba, wx, bx, lam):
    B, S, _ = xr.shape
    xc = lax.conv_general_dilated(xr, conv_w, window_strides=(1,), padding=((CONV_W - 1, 0),),
                                  dimension_numbers=('NWC', 'WIO', 'NWC'),
                                  feature_group_count=D_LRU) + conv_b
    xh = xc.reshape(B, S, LRU_HEADS, LRU_HD)
    r = jax.nn.sigmoid(jnp.einsum('bshi,hij->bshj', xh, wa) + ba).reshape(B, S, D_LRU)
    i = jax.nn.sigmoid(jnp.einsum('bshi,hij->bshj', xh, wx) + bx).reshape(B, S, D_LRU)
    log_a = -LRU_C * jax.nn.softplus(-lam.astype(jnp.float32)) * r.astype(jnp.float32)
    a = jnp.exp(log_a)
    b = jnp.sqrt(-jnp.expm1(2.0 * log_a)) * (i * xc).astype(jnp.float32)
    _, h = lax.associative_scan(_lin_rec_combine, (a, b), axis=1)
    return h.astype(xr.dtype) * jax.nn.gelu(yg)


def peer(h, w_q, k1, k2, u, v):
    B, S, D = h.shape
    T = B * S
    half = PEER_DKEY // 2
    hf = h.reshape(T, D)
    q = (hf @ w_q).reshape(T, PEER_HEADS, PEER_DKEY).astype(jnp.float32)
    q = (q - jnp.mean(q, -1, keepdims=True)) * lax.rsqrt(jnp.var(q, -1, keepdims=True) + LN_EPS)
    s1 = jnp.einsum('thd,nd->thn', q[..., :half], k1.astype(jnp.float32))
    s2 = jnp.einsum('thd,nd->thn', q[..., half:], k2.astype(jnp.float32))
    v1, i1 = lax.top_k(s1, PEER_TOPK)
    v2, i2 = lax.top_k(s2, PEER_TOPK)
    cand = (v1[..., :, None] + v2[..., None, :]).reshape(T, PEER_HEADS, PEER_TOPK * PEER_TOPK)
    vs, j = lax.top_k(cand, PEER_TOPK)
    e = (jnp.take_along_axis(i1, j // PEER_TOPK, axis=-1) * PEER_NKEYS
         + jnp.take_along_axis(i2, j % PEER_TOPK, axis=-1))
    g = jax.nn.softmax(vs, axis=-1).astype(h.dtype)
    nb = T // PEER_BLOCK
    hk = PEER_HEADS * PEER_TOPK
    xb_all = hf.reshape(nb, PEER_BLOCK, D)
    eb_all = e.reshape(nb, PEER_BLOCK, hk)
    gb_all = g.reshape(nb, PEER_BLOCK, hk)

    def block(args):
        xb, eb, gb = args
        act = jax.nn.gelu(jnp.einsum('td,tkd->tk', xb, u[eb]))
        return jnp.einsum('tk,tkd->td', gb * act, v[eb])

    out = lax.map(block, (xb_all, eb_all, gb_all))
    return out.reshape(B, S, D)


def setup_inputs(seed: int = 0) -> dict:
    key = jax.random.key(seed)
    ks = jax.random.split(key, 32)
    f32 = jnp.float32
    D = D_MODEL
    L = DEPTH

    def nrm(k, shape, s):
        return jax.random.normal(k, shape, f32) * s

    gate_slots = jnp.repeat(jnp.array([0.0, 0.0, 1.0, 0.0, 0.0, 1.0], f32), D)
    a_c = jax.random.uniform(ks[15], (L, D_LRU), f32, 0.9, 0.999)
    a0 = a_c ** (1.0 / LRU_C)
    return {
        "x": nrm(ks[0], (BATCH, SEQ, D), 1.0),
        "c": nrm(ks[1], (BATCH, D), 1.0),
        "ln0_g": 1.0 + nrm(ks[2], (D,), 0.02),
        "ln0_b": nrm(ks[3], (D,), 0.02),
        "ada_w": nrm(ks[4], (L, D, 6 * D), 0.2 * D ** -0.5),
        "ada_b": gate_slots + nrm(ks[5], (L, 6 * D), 0.02),
        "w_in": nrm(ks[6], (L, D, D_POOL + 2 * D_LRU), D ** -0.5),
        "pool_w": nrm(ks[7], (L, POOL_GROUPS, POOL_GW, POOL_GW), POOL_GW ** -0.5),
        "pool_scale": 1.0 + nrm(ks[8], (L, D_POOL), 0.02),
        "conv_w": nrm(ks[9], (L, CONV_W, 1, D_LRU), CONV_W ** -0.5),
        "conv_b": nrm(ks[10], (L, D_LRU), 0.02),
        "lru_wa": nrm(ks[11], (L, LRU_HEADS, LRU_HD, LRU_HD), LRU_HD ** -0.5),
        "lru_ba": nrm(ks[12], (L, LRU_HEADS, LRU_HD), 0.02),
        "lru_wx": nrm(ks[13], (L, LRU_HEADS, LRU_HD, LRU_HD), LRU_HD ** -0.5),
        "lru_bx": nrm(ks[14], (L, LRU_HEADS, LRU_HD), 0.02),
        "lru_lambda": jnp.log(a0) - jnp.log1p(-a0),
        "w_out": nrm(ks[16], (L, D_MIX, D), BETA * D_MIX ** -0.5),
        "ln1_g": 1.0 + nrm(ks[17], (L, D), 0.02),
        "ln1_b": nrm(ks[18], (L, D), 0.02),
        "peer_wq": nrm(ks[19], (L, D, PEER_HEADS * PEER_DKEY), D ** -0.5),
        "peer_k1": nrm(ks[20], (L, PEER_NKEYS, PEER_DKEY // 2), (PEER_DKEY // 2) ** -0.5),
        "peer_k2": nrm(ks[21], (L, PEER_NKEYS, PEER_DKEY // 2), (PEER_DKEY // 2) ** -0.5),
        "peer_u": nrm(ks[22], (L, PEER_N, D), D ** -0.5),
        "peer_v": nrm(ks[23], (L, PEER_N, D), BETA),
        "ln2_g": 1.0 + nrm(ks[24], (L, D), 0.02),
        "ln2_b": nrm(ks[25], (L, D), 0.02),
    }


def reference(x, c, ln0_g, ln0_b, ada_w, ada_b, w_in, pool_w, pool_scale, conv_w, conv_b,
              lru_wa, lru_ba, lru_wx, lru_bx, lru_lambda, w_out, ln1_g, ln1_b,
              peer_wq, peer_k1, peer_k2, peer_u, peer_v, ln2_g, ln2_b):
    x = layer_norm(x, ln0_g, ln0_b)
    cond = jax.nn.silu(c)
    for l in range(DEPTH):
        mod = cond @ ada_w[l] + ada_b[l]
        sh1, sc1, g1, sh2, sc2, g2 = [m[:, None, :] for m in jnp.split(mod, 6, axis=-1)]
        h = x * (1.0 + sc1) + sh1
        z = h @ w_in[l]
        zp = z[..., :D_POOL]
        zx = z[..., D_POOL:D_POOL + D_LRU]
        zg = z[..., D_POOL + D_LRU:]
        y_pool = pool_mixer(zp, pool_w[l], pool_scale[l])
        y_lru = rglru_mixer(zx, zg, conv_w[l], conv_b[l], lru_wa[l], lru_ba[l],
                            lru_wx[l], lru_bx[l], lru_lambda[l])
        y = jnp.concatenate([y_pool, y_lru], axis=-1) @ w_out[l]
        x = layer_norm(ALPHA * x + g1 * y, ln1_g[l], ln1_b[l])
        h = x * (1.0 + sc2) + sh2
        y = peer(h, peer_wq[l], peer_k1[l], peer_k2[l], peer_u[l], peer_v[l])
        x = layer_norm(ALPHA * x + g2 * y, ln2_g[l], ln2_b[l])
    return x
```

```python
import functools

import jax
import jax.numpy as jnp
from jax import lax
from jax.experimental import pallas as pl
from jax.experimental.pallas import tpu as pltpu

F32 = jnp.float32
BF16 = jnp.bfloat16

D_MODEL = 1024
D_POOL = 512
D_LRU = 512
POOL_WINDOWS = (2, 4, 8, 16)
POOL_GW = 128
POOL_HALO = 16
LRU_HEADS = 8
LRU_HD = 64
CONV_W = 4
CONV_HALO = 8
LRU_C = 8.0
PEER_HEADS = 8
PEER_NKEYS = 128
PEER_DKEY = 256
PEER_TOPK = 16
PEER_N = PEER_NKEYS * PEER_NKEYS
LN_EPS = 1e-5
DEPTH = 1
ALPHA = (2.0 * DEPTH) ** 0.25

LANES = 128
SUBLANES = 8
VMEM_LIMIT = 56 * 1024 * 1024

NEG_INF = float("-inf")


def _layer_norm(x, g, b):
    mu = jnp.mean(x, axis=-1, keepdims=True)
    xc = x - mu
    var = jnp.mean(xc * xc, axis=-1, keepdims=True)
    return xc * lax.rsqrt(var + LN_EPS) * g + b


def _gelu(x):
    return 0.5 * x * (1.0 + jnp.tanh(0.7978845608028654 * (x + 0.044715 * (x * x * x))))


def _sigmoid(x):
    return 1.0 / (1.0 + jnp.exp(-x))


def _mod_kernel(c_ref, w_ref, b_ref, o_ref):
    c = c_ref[...]
    cond = c * _sigmoid(c)
    o_ref[...] = jnp.dot(cond, w_ref[...], preferred_element_type=F32) + b_ref[...]


def _mod_call(c_pad, ada_w, ada_b, tn=1024):
    rows, d = c_pad.shape
    n = ada_w.shape[1]
    return pl.pallas_call(
        _mod_kernel,
        grid=(n // tn,),
        in_specs=[pl.BlockSpec((rows, d), lambda j: (0, 0)),
                  pl.BlockSpec((d, tn), lambda j: (0, j)),
                  pl.BlockSpec((1, tn), lambda j: (0, j))],
        out_specs=pl.BlockSpec((rows, tn), lambda j: (0, j)),
        out_shape=jax.ShapeDtypeStruct((rows, n), F32),
        name="mod",
    )(c_pad, ada_w, ada_b)


def _mixer_kernel(x_ref, mod_ref, g0_ref, b0_ref, win_ref, pw_ref, ps_ref, cw_ref, cb_ref,
                  wg_ref, bg_ref, lam_ref, x0_ref, ym_ref,
                  zp_ext, zx_ext, a_s, b_s, h_s, hc_ref, *, ts):
    s = pl.program_id(1)
    x0 = _layer_norm(x_ref[0], g0_ref[...], b0_ref[...])
    x0_ref[0] = x0
    mod = mod_ref[0]
    h = x0 * (1.0 + mod[1:2]) + mod[0:1]
    z = jnp.dot(h.astype(BF16), win_ref[...], preferred_element_type=F32)

    @pl.when(s == 0)
    def _():
        zp_ext[0:POOL_HALO, :] = jnp.zeros((POOL_HALO, D_POOL), F32)
        zx_ext[0:CONV_HALO, :] = jnp.zeros((CONV_HALO, D_LRU), F32)
        hc_ref[...] = jnp.zeros_like(hc_ref)

    @pl.when(s > 0)
    def _():
        zp_ext[0:POOL_HALO, :] = zp_ext[ts:ts + POOL_HALO, :]
        zx_ext[0:CONV_HALO, :] = zx_ext[ts:ts + CONV_HALO, :]

    zp_ext[POOL_HALO:, :] = z[:, :D_POOL]
    zx_ext[CONV_HALO:, :] = z[:, D_POOL:D_POOL + D_LRU]
    zg = z[:, D_POOL + D_LRU:]

    t_glob = lax.broadcasted_iota(jnp.int32, (ts, POOL_GW), 0) + s * ts
    outs = []
    for g, w in enumerate(POOL_WINDOWS):
        cs = slice(g * POOL_GW, (g + 1) * POOL_GW)
        cur = zp_ext[POOL_HALO:POOL_HALO + ts, cs]
        acc = cur
        for k in range(1, w):
            acc = acc + zp_ext[POOL_HALO - k:POOL_HALO - k + ts, cs]
        cnt = jnp.minimum(t_glob + 1, w).astype(F32)
        y = acc / cnt - cur
        outs.append(jnp.dot(y.astype(BF16), pw_ref[g], preferred_element_type=F32))
    y_pool = jnp.concatenate(outs, axis=1) * ps_ref[...]

    cw = cw_ref[...]
    xc = cb_ref[...] + cw[0:1] * zx_ext[CONV_HALO - 3:CONV_HALO - 3 + ts, :]
    for k in range(1, CONV_W):
        xc = xc + cw[k:k + 1] * zx_ext[CONV_HALO - 3 + k:CONV_HALO - 3 + k + ts, :]
    gates = jnp.dot(xc.astype(BF16), wg_ref[...], preferred_element_type=F32) + bg_ref[...]
    r = _sigmoid(gates[:, :D_LRU])
    i = _sigmoid(gates[:, D_LRU:])
    nl = -lam_ref[...]
    softplus = jnp.maximum(nl, 0.0) + jnp.log(1.0 + jnp.exp(-jnp.abs(nl)))
    a = jnp.exp((-LRU_C * softplus) * r)
    bv = jnp.sqrt(jnp.maximum(1.0 - a * a, 0.0)) * (i * xc)

    row_in_grp = lax.broadcasted_iota(jnp.int32, (ts, D_LRU), 0) % SUBLANES
    for k in (1, 2, 4):
        a_sh = pltpu.roll(a, k, 0)
        b_sh = pltpu.roll(bv, k, 0)
        valid = row_in_grp >= k
        bv = jnp.where(valid, a * b_sh + bv, bv)
        a = jnp.where(valid, a * a_sh, a)
    a_s[...] = a
    b_s[...] = bv

    def grp(gi, hprev):
        off = pl.multiple_of(gi * SUBLANES, SUBLANES)
        hg = a_s[pl.ds(off, SUBLANES), :] * hprev + b_s[pl.ds(off, SUBLANES), :]
        h_s[pl.ds(off, SUBLANES), :] = hg
        return hg[SUBLANES - 1:SUBLANES, :]

    hc_ref[0:1, :] = lax.fori_loop(0, ts // SUBLANES, grp, hc_ref[0:1, :])
    y_lru = h_s[...] * _gelu(zg)
    ym_ref[0] = jnp.concatenate([y_pool, y_lru], axis=1).astype(BF16)


def _mixer_call(x, mod3, ln0_g, ln0_b, w_in, pool_w, pool_scale, conv_w, conv_b, wg, bg, lam, ts):
    B, S, D = x.shape
    dz = w_in.shape[1]
    full = lambda shape: pl.BlockSpec(shape, lambda b, s: (0,) * len(shape))
    return pl.pallas_call(
        functools.partial(_mixer_kernel, ts=ts),
        grid=(B, S // ts),
        in_specs=[pl.BlockSpec((1, ts, D), lambda b, s: (b, s, 0)),
                  pl.BlockSpec((1, 6, D), lambda b, s: (b, 0, 0)),
                  full((1, D)), full((1, D)), full((D, dz)),
                  full((len(POOL_WINDOWS), POOL_GW, POOL_GW)), full((1, D_POOL)),
                  full((CONV_W, D_LRU)), full((1, D_LRU)),
                  full((D_LRU, 2 * D_LRU)), full((1, 2 * D_LRU)), full((1, D_LRU))],
        out_specs=[pl.BlockSpec((1, ts, D), lambda b, s: (b, s, 0)),
                   pl.BlockSpec((1, ts, D), lambda b, s: (b, s, 0))],
        out_shape=[jax.ShapeDtypeStruct((B, S, D), F32),
                   jax.ShapeDtypeStruct((B, S, D), BF16)],
        scratch_shapes=[pltpu.VMEM((POOL_HALO + ts, D_POOL), F32),
                        pltpu.VMEM((CONV_HALO + ts, D_LRU), F32),
                        pltpu.VMEM((ts, D_LRU), F32),
                        pltpu.VMEM((ts, D_LRU), F32),
                        pltpu.VMEM((ts, D_LRU), F32),
                        pltpu.VMEM((SUBLANES, D_LRU), F32)],
        compiler_params=pltpu.CompilerParams(
            dimension_semantics=("arbitrary", "arbitrary"), vmem_limit_bytes=VMEM_LIMIT),
        name="mixer",
    )(x, mod3, ln0_g, ln0_b, w_in, pool_w, pool_scale, conv_w, conv_b, wg, bg, lam)


def _top16(s):
    work = s
    rank = jnp.full(s.shape, float(PEER_TOPK), F32)
    vals = []
    for r in range(PEER_TOPK):
        m = jnp.max(work, axis=0, keepdims=True)
        eq = work == m
        rank = jnp.where(eq, float(r), rank)
        work = jnp.where(eq, NEG_INF, work)
        vals.append(m)
    return rank, vals


def _route_kernel(ym_ref, x0_ref, mod_ref, wout_ref, g1_ref, b1_ref, wq_ref, k1_ref, k2_ref,
                  x1_ref, h2t_ref, rank2_ref, e2_ref, n1_ref, e1_ref, s_ref, *, ts):
    nlt = ts // LANES
    mod = mod_ref[0]
    y = jnp.dot(ym_ref[0], wout_ref[...], preferred_element_type=F32)
    x1 = _layer_norm(ALPHA * x0_ref[0] + mod[2:3] * y, g1_ref[...], b1_ref[...])
    x1_ref[0] = x1
    h2 = x1 * (1.0 + mod[4:5]) + mod[3:4]
    h2t_ref[...] = h2.T.astype(BF16)
    q = jnp.dot(h2.astype(BF16), wq_ref[...], preferred_element_type=F32)
    half = PEER_DKEY // 2
    nt = (((1,), (1,)), ((), ()))
    for hh in range(PEER_HEADS):
        qh = q[:, hh * PEER_DKEY:(hh + 1) * PEER_DKEY]
        mu = jnp.mean(qh, axis=-1, keepdims=True)
        qc = qh - mu
        var = jnp.mean(qc * qc, axis=-1, keepdims=True)
        qb = (qc * lax.rsqrt(var + LN_EPS)).astype(BF16)
        s1t = lax.dot_general(k1_ref[...], qb[:, :half], nt, preferred_element_type=F32)
        s2t = lax.dot_general(k2_ref[...], qb[:, half:], nt, preferred_element_type=F32)
        for lt in range(nlt):
            s_ref[0, hh, lt] = s1t[:, lt * LANES:(lt + 1) * LANES]
            s_ref[1, hh, lt] = s2t[:, lt * LANES:(lt + 1) * LANES]

    row16 = lax.broadcasted_iota(jnp.int32, (PEER_TOPK, LANES), 0)

    def sel(idx, carry):
        hh = idx // nlt
        lt = idx % nlt
        s1 = s_ref[0, hh, lt]
        s2 = s_ref[1, hh, lt]
        rank1, vals1 = _top16(s1)
        rank2, vals2 = _top16(s2)
        v1 = jnp.zeros((PEER_TOPK, LANES), F32)
        for r in range(PEER_TOPK):
            v1 = jnp.where(row16 == r, vals1[r], v1)
        blocks = [v1[0:8] + vals2[0], v1[8:16] + vals2[0]]
        blocks += [v1[0:8] + vals2[r] for r in range(1, PEER_TOPK)]
        vs = []
        for _ in range(PEER_TOPK):
            m = blocks[0]
            for blk in blocks[1:]:
                m = jnp.maximum(m, blk)
            m = jnp.max(m, axis=0, keepdims=True)
            vs.append(m)
            blocks = [jnp.where(blk == m, NEG_INF, blk) for blk in blocks]
        tau = vs[PEER_TOPK - 1]
        zsum = jnp.ones_like(tau)
        for v in vs[1:]:
            zsum = zsum + jnp.exp(v - vs[0])
        n = jnp.zeros((PEER_TOPK, LANES), F32)
        for r in range(PEER_TOPK):
            n = n + jnp.where(v1 + vals2[r] >= tau, 1.0, 0.0)
        n1 = jnp.zeros((PEER_NKEYS, LANES), F32)
        for r in range(PEER_TOPK):
            n1 = jnp.where(rank1 == float(r), n[r:r + 1], n1)
        rank2_ref[hh, lt] = rank2
        n1_ref[hh, lt] = n1
        e1_ref[hh, lt] = jnp.exp(s1 - vals1[0])
        e2_ref[hh, lt] = jnp.exp(s2 - vals2[0]) / zsum
        return carry

    lax.fori_loop(0, PEER_HEADS * nlt, sel, 0)


def _route_call(ymix, x0, mod3, w_out, ln1_g, ln1_b, w_q, k1, k2, ts):
    B, S, D = x0.shape
    T = B * S
    nS = S // ts
    nlt = ts // LANES
    dq = w_q.shape[1]
    full = lambda shape: pl.BlockSpec(shape, lambda b, s: (0,) * len(shape))
    tab = jax.ShapeDtypeStruct((PEER_HEADS, T // LANES, PEER_NKEYS, LANES), F32)
    tab_spec = pl.BlockSpec((PEER_HEADS, nlt, PEER_NKEYS, LANES), lambda b, s: (0, b * nS + s, 0, 0))
    return pl.pallas_call(
        functools.partial(_route_kernel, ts=ts),
        grid=(B, nS),
        in_specs=[pl.BlockSpec((1, ts, D), lambda b, s: (b, s, 0)),
                  pl.BlockSpec((1, ts, D), lambda b, s: (b, s, 0)),
                  pl.BlockSpec((1, 6, D), lambda b, s: (b, 0, 0)),
                  full((D, D)), full((1, D)), full((1, D)), full((D, dq)),
                  full((PEER_NKEYS, PEER_DKEY // 2)), full((PEER_NKEYS, PEER_DKEY // 2))],
        out_specs=[pl.BlockSpec((1, ts, D), lambda b, s: (b, s, 0)),
                   pl.BlockSpec((D, ts), lambda b, s: (0, b * nS + s)),
                   tab_spec, tab_spec, tab_spec, tab_spec],
        out_shape=[jax.ShapeDtypeStruct((B, S, D), F32),
                   jax.ShapeDtypeStruct((D, T), BF16),
                   tab, tab, tab, tab],
        scratch_shapes=[pltpu.VMEM((2, PEER_HEADS, nlt, PEER_NKEYS, LANES), F32)],
        compiler_params=pltpu.CompilerParams(
            dimension_semantics=("arbitrary", "arbitrary"), vmem_limit_bytes=VMEM_LIMIT),
        name="route",
    )(ymix, x0, mod3, w_out, ln1_g, ln1_b, w_q, k1, k2)


def _peer_kernel(h2t_ref, u_ref, vt_ref, rank2_ref, e2_ref, n1_ref, e1_ref, x1_ref, mod_ref,
                 g2_ref, b2_ref, o_ref, acc_ref, p_ref, act_ref, *, tm, te):
    j = pl.program_id(1)
    nct = tm // LANES
    nblk = te // PEER_NKEYS

    @pl.when(j == 0)
    def _():
        acc_ref[...] = jnp.zeros_like(acc_ref)

    act_ref[...] = _gelu(jnp.dot(u_ref[...], h2t_ref[...], preferred_element_type=F32))

    def chunk(c, carry):
        co = pl.multiple_of(c * LANES, LANES)
        for b in range(nblk):
            w = jnp.zeros((PEER_NKEYS, LANES), F32)
            for hh in range(PEER_HEADS):
                keep = rank2_ref[hh, c] < n1_ref[hh, c, b:b + 1, :]
                w = w + jnp.where(keep, e2_ref[hh, c], 0.0) * e1_ref[hh, c, b:b + 1, :]
            rs = slice(b * PEER_NKEYS, (b + 1) * PEER_NKEYS)
            p_ref[rs, pl.ds(co, LANES)] = (w * act_ref[rs, pl.ds(co, LANES)]).astype(BF16)
        return carry

    lax.fori_loop(0, nct, chunk, 0)
    acc_ref[...] += jnp.dot(vt_ref[...], p_ref[...], preferred_element_type=F32)

    @pl.when(j == pl.num_programs(1) - 1)
    def _():
        mod = mod_ref[0]
        y = acc_ref[...].T
        o_ref[...] = _layer_norm(ALPHA * x1_ref[...] + mod[5:6] * y, g2_ref[...], b2_ref[...])


def _peer_call(h2t, u_b, vt_b, rank2, e2, n1, e1, x1f, mod3, ln2_g, ln2_b, S, tm, te):
    D, T = h2t.shape
    ne = u_b.shape[0]
    nct = tm // LANES
    nblk = te // PEER_NKEYS
    tiles_per_batch = S // tm
    full = lambda shape: pl.BlockSpec(shape, lambda i, j: (0,) * len(shape))
    tok_tab = pl.BlockSpec((PEER_HEADS, nct, PEER_NKEYS, LANES), lambda i, j: (0, i, 0, 0))
    key_tab = pl.BlockSpec((PEER_HEADS, nct, nblk, LANES), lambda i, j: (0, i, j, 0))
    return pl.pallas_call(
        functools.partial(_peer_kernel, tm=tm, te=te),
        grid=(T // tm, ne // te),
        in_specs=[pl.BlockSpec((D, tm), lambda i, j: (0, i)),
                  pl.BlockSpec((te, D), lambda i, j: (j, 0)),
                  pl.BlockSpec((D, te), lambda i, j: (0, j)),
                  tok_tab, tok_tab, key_tab, key_tab,
                  pl.BlockSpec((tm, D), lambda i, j: (i, 0)),
                  pl.BlockSpec((1, 6, D), lambda i, j: (i // tiles_per_batch, 0, 0)),
                  full((1, D)), full((1, D))],
        out_specs=pl.BlockSpec((tm, D), lambda i, j: (i, 0)),
        out_shape=jax.ShapeDtypeStruct((T, D), F32),
        scratch_shapes=[pltpu.VMEM((D, tm), F32),
                        pltpu.VMEM((te, tm), BF16),
                        pltpu.VMEM((te, tm), F32)],
        compiler_params=pltpu.CompilerParams(
            dimension_semantics=("arbitrary", "arbitrary"), vmem_limit_bytes=VMEM_LIMIT),
        name="peer",
    )(h2t, u_b, vt_b, rank2, e2, n1, e1, x1f, mod3, ln2_g, ln2_b)


def _block_diag(w):
    nh, hd, _ = w.shape
    eye = jnp.eye(nh, dtype=w.dtype)
    return (eye[:, None, :, None] * w[:, :, None, :]).reshape(nh * hd, nh * hd)


def kernel(x, c, ln0_g, ln0_b, ada_w, ada_b, w_in, pool_w, pool_scale, conv_w, conv_b, lru_wa, lru_ba, lru_wx, lru_bx, lru_lambda, w_out, ln1_g, ln1_b, peer_wq, peer_k1, peer_k2, peer_u, peer_v, ln2_g, ln2_b, *, ts=512, tm=512, te=1024):
    B, S, D = x.shape
    T = B * S
    row = lambda v: v.reshape(1, -1)
    l = 0
    c_pad = jnp.pad(c, ((0, SUBLANES - B), (0, 0)))
    mod = _mod_call(c_pad, ada_w[l], row(ada_b[l]))[:B]
    mod3 = mod.reshape(B, 6, D)

    wg = jnp.concatenate([_block_diag(lru_wa[l]), _block_diag(lru_wx[l])], axis=1).astype(BF16)
    bg = jnp.concatenate([lru_ba[l].reshape(1, -1), lru_bx[l].reshape(1, -1)], axis=1)
    x0, ymix = _mixer_call(x, mod3, row(ln0_g), row(ln0_b), w_in[l].astype(BF16),
                           pool_w[l].astype(BF16), row(pool_scale[l]),
                           conv_w[l].reshape(CONV_W, D_LRU), row(conv_b[l]), wg, bg,
                           row(lru_lambda[l]), ts)
    x1, h2t, rank2, e2, n1, e1 = _route_call(
        ymix, x0, mod3, w_out[l].astype(BF16), row(ln1_g[l]), row(ln1_b[l]),
        peer_wq[l].astype(BF16), peer_k1[l].astype(BF16), peer_k2[l].astype(BF16), ts)
    out = _peer_call(h2t, peer_u[l].astype(BF16), peer_v[l].T.astype(BF16), rank2, e2, n1, e1,
                     x1.reshape(T, D), mod3, row(ln2_g[l]), row(ln2_b[l]), S, tm, te)
    return out.reshape(B, S, D)
```

```python
import functools

import jax
import jax.numpy as jnp
from jax import lax
from jax.experimental import pallas as pl
from jax.experimental.pallas import tpu as pltpu

F32 = jnp.float32
BF16 = jnp.bfloat16

D_MODEL = 1024
D_POOL = 512
D_LRU = 512
POOL_WINDOWS = (2, 4, 8, 16)
POOL_GW = 128
POOL_HALO = 16
LRU_HEADS = 8
LRU_HD = 64
CONV_W = 4
CONV_HALO = 8
LRU_C = 8.0
PEER_HEADS = 8
PEER_NKEYS = 128
PEER_DKEY = 256
PEER_TOPK = 16
PEER_N = PEER_NKEYS * PEER_NKEYS
LN_EPS = 1e-5
DEPTH = 1
ALPHA = (2.0 * DEPTH) ** 0.25

LANES = 128
SUBLANES = 8
BF16_ROWS = 2 * SUBLANES
NPIECE = PEER_NKEYS // BF16_ROWS
TOK_CHUNK = 2 * LANES
PIECE_GROUP = 8
BLK_GROUP = 2
VMEM_LIMIT = 56 * 1024 * 1024

NEG_INF = float("-inf")


def _layer_norm(x, g, b):
    mu = jnp.mean(x, axis=-1, keepdims=True)
    xc = x - mu
    var = jnp.mean(xc * xc, axis=-1, keepdims=True)
    return xc * lax.rsqrt(var + LN_EPS) * g + b


def _gelu(x):
    return 0.5 * x * (1.0 + jnp.tanh(0.7978845608028654 * (x + 0.044715 * (x * x * x))))


def _sigmoid(x):
    return 1.0 / (1.0 + jnp.exp(-x))


def _mod_kernel(c_ref, w_ref, b_ref, o_ref):
    c = c_ref[...]
    cond = c * _sigmoid(c)
    o_ref[...] = jnp.dot(cond, w_ref[...], preferred_element_type=F32) + b_ref[...]


def _mod_call(c_pad, ada_w, ada_b, tn=1024):
    rows, d = c_pad.shape
    n = ada_w.shape[1]
    return pl.pallas_call(
        _mod_kernel,
        grid=(n // tn,),
        in_specs=[pl.BlockSpec((rows, d), lambda j: (0, 0)),
                  pl.BlockSpec((d, tn), lambda j: (0, j)),
                  pl.BlockSpec((1, tn), lambda j: (0, j))],
        out_specs=pl.BlockSpec((rows, tn), lambda j: (0, j)),
        out_shape=jax.ShapeDtypeStruct((rows, n), F32),
        name="mod",
    )(c_pad, ada_w, ada_b)


def _mixer_kernel(x_ref, mod_ref, g0_ref, b0_ref, win_ref, pw_ref, ps_ref, cw_ref, cb_ref,
                  wg_ref, bg_ref, lam_ref, x0_ref, ym_ref,
                  zp_ext, zx_ext, a_s, b_s, h_s, hc_ref, *, ts):
    s = pl.program_id(1)
    x0 = _layer_norm(x_ref[0], g0_ref[...], b0_ref[...])
    x0_ref[0] = x0
    mod = mod_ref[0]
    h = x0 * (1.0 + mod[1:2]) + mod[0:1]
    z = jnp.dot(h.astype(BF16), win_ref[...], preferred_element_type=F32)

    @pl.when(s == 0)
    def _():
        zp_ext[0:POOL_HALO, :] = jnp.zeros((POOL_HALO, D_POOL), F32)
        zx_ext[0:CONV_HALO, :] = jnp.zeros((CONV_HALO, D_LRU), F32)
        hc_ref[...] = jnp.zeros_like(hc_ref)

    @pl.when(s > 0)
    def _():
        zp_ext[0:POOL_HALO, :] = zp_ext[ts:ts + POOL_HALO, :]
        zx_ext[0:CONV_HALO, :] = zx_ext[ts:ts + CONV_HALO, :]

    zp_ext[POOL_HALO:, :] = z[:, :D_POOL]
    zx_ext[CONV_HALO:, :] = z[:, D_POOL:D_POOL + D_LRU]
    zg = z[:, D_POOL + D_LRU:]

    t_glob = lax.broadcasted_iota(jnp.int32, (ts, POOL_GW), 0) + s * ts
    outs = []
    for g, w in enumerate(POOL_WINDOWS):
        cs = slice(g * POOL_GW, (g + 1) * POOL_GW)
        cur = zp_ext[POOL_HALO:POOL_HALO + ts, cs]
        acc = cur
        for k in range(1, w):
            acc = acc + zp_ext[POOL_HALO - k:POOL_HALO - k + ts, cs]
        cnt = jnp.minimum(t_glob + 1, w).astype(F32)
        y = acc / cnt - cur
        outs.append(jnp.dot(y.astype(BF16), pw_ref[g], preferred_element_type=F32))
    y_pool = jnp.concatenate(outs, axis=1) * ps_ref[...]

    cw = cw_ref[...]
    xc = cb_ref[...] + cw[0:1] * zx_ext[CONV_HALO - 3:CONV_HALO - 3 + ts, :]
    for k in range(1, CONV_W):
        xc = xc + cw[k:k + 1] * zx_ext[CONV_HALO - 3 + k:CONV_HALO - 3 + k + ts, :]
    gates = jnp.dot(xc.astype(BF16), wg_ref[...], preferred_element_type=F32) + bg_ref[...]
    r = _sigmoid(gates[:, :D_LRU])
    i = _sigmoid(gates[:, D_LRU:])
    nl = -lam_ref[...]
    softplus = jnp.maximum(nl, 0.0) + jnp.log(1.0 + jnp.exp(-jnp.abs(nl)))
    a = jnp.exp((-LRU_C * softplus) * r)
    bv = jnp.sqrt(jnp.maximum(1.0 - a * a, 0.0)) * (i * xc)

    row_in_grp = lax.broadcasted_iota(jnp.int32, (ts, D_LRU), 0) % SUBLANES
    for k in (1, 2, 4):
        a_sh = pltpu.roll(a, k, 0)
        b_sh = pltpu.roll(bv, k, 0)
        valid = row_in_grp >= k
        bv = jnp.where(valid, a * b_sh + bv, bv)
        a = jnp.where(valid, a * a_sh, a)
    a_s[...] = a
    b_s[...] = bv

    def grp(gi, hprev):
        off = pl.multiple_of(gi * SUBLANES, SUBLANES)
        hg = a_s[pl.ds(off, SUBLANES), :] * hprev + b_s[pl.ds(off, SUBLANES), :]
        h_s[pl.ds(off, SUBLANES), :] = hg
        return hg[SUBLANES - 1:SUBLANES, :]

    hc_ref[0:1, :] = lax.fori_loop(0, ts // SUBLANES, grp, hc_ref[0:1, :])
    y_lru = h_s[...] * _gelu(zg)
    ym_ref[0] = jnp.concatenate([y_pool, y_lru], axis=1).astype(BF16)


def _mixer_call(x, mod3, ln0_g, ln0_b, w_in, pool_w, pool_scale, conv_w, conv_b, wg, bg, lam, ts):
    B, S, D = x.shape
    dz = w_in.shape[1]
    full = lambda shape: pl.BlockSpec(shape, lambda b, s: (0,) * len(shape))
    return pl.pallas_call(
        functools.partial(_mixer_kernel, ts=ts),
        grid=(B, S // ts),
        in_specs=[pl.BlockSpec((1, ts, D), lambda b, s: (b, s, 0)),
                  pl.BlockSpec((1, 6, D), lambda b, s: (b, 0, 0)),
                  full((1, D)), full((1, D)), full((D, dz)),
                  full((len(POOL_WINDOWS), POOL_GW, POOL_GW)), full((1, D_POOL)),
                  full((CONV_W, D_LRU)), full((1, D_LRU)),
                  full((D_LRU, 2 * D_LRU)), full((1, 2 * D_LRU)), full((1, D_LRU))],
        out_specs=[pl.BlockSpec((1, ts, D), lambda b, s: (b, s, 0)),
                   pl.BlockSpec((1, ts, D), lambda b, s: (b, s, 0))],
        out_shape=[jax.ShapeDtypeStruct((B, S, D), F32),
                   jax.ShapeDtypeStruct((B, S, D), BF16)],
        scratch_shapes=[pltpu.VMEM((POOL_HALO + ts, D_POOL), F32),
                        pltpu.VMEM((CONV_HALO + ts, D_LRU), F32),
                        pltpu.VMEM((ts, D_LRU), F32),
                        pltpu.VMEM((ts, D_LRU), F32),
                        pltpu.VMEM((ts, D_LRU), F32),
                        pltpu.VMEM((SUBLANES, D_LRU), F32)],
        compiler_params=pltpu.CompilerParams(
            dimension_semantics=("arbitrary", "arbitrary"), vmem_limit_bytes=VMEM_LIMIT),
        name="mixer",
    )(x, mod3, ln0_g, ln0_b, w_in, pool_w, pool_scale, conv_w, conv_b, wg, bg, lam)


def _top16(s, with_rank):
    work = s
    rank = jnp.full(s.shape, float(PEER_TOPK), F32) if with_rank else None
    vals = []
    for r in range(PEER_TOPK):
        m = jnp.max(work, axis=0, keepdims=True)
        eq = work == m
        if with_rank:
            rank = jnp.where(eq, float(r), rank)
        work = jnp.where(eq, NEG_INF, work)
        vals.append(m)
    return rank, vals


def _route_kernel(ym_ref, x0_ref, mod_ref, wout_ref, g1_ref, b1_ref, wq_ref, k1_ref, k2_ref,
                  x1_ref, h2t_ref, rank2_ref, e2_ref, n1_ref, e1_ref, s_ref, *, ts):
    nlt = ts // LANES
    mod = mod_ref[0]
    y = jnp.dot(ym_ref[0], wout_ref[...], preferred_element_type=F32)
    x1 = _layer_norm(ALPHA * x0_ref[0] + mod[2:3] * y, g1_ref[...], b1_ref[...])
    x1_ref[0] = x1
    h2 = x1 * (1.0 + mod[4:5]) + mod[3:4]
    h2t_ref[...] = h2.T.astype(BF16)
    q = jnp.dot(h2.astype(BF16), wq_ref[...], preferred_element_type=F32)
    half = PEER_DKEY // 2
    nt = (((1,), (1,)), ((), ()))
    for hh in range(PEER_HEADS):
        qh = q[:, hh * PEER_DKEY:(hh + 1) * PEER_DKEY]
        mu = jnp.mean(qh, axis=-1, keepdims=True)
        qc = qh - mu
        var = jnp.mean(qc * qc, axis=-1, keepdims=True)
        qb = (qc * lax.rsqrt(var + LN_EPS)).astype(BF16)
        s1t = lax.dot_general(k1_ref[...], qb[:, :half], nt, preferred_element_type=F32)
        s2t = lax.dot_general(k2_ref[...], qb[:, half:], nt, preferred_element_type=F32)
        for lt in range(nlt):
            s_ref[0, hh, lt] = s1t[:, lt * LANES:(lt + 1) * LANES]
            s_ref[1, hh, lt] = s2t[:, lt * LANES:(lt + 1) * LANES]

    row8 = lax.broadcasted_iota(jnp.int32, (SUBLANES, LANES), 0).astype(F32)

    def sel(idx, carry):
        hh = idx // nlt
        lt = idx % nlt
        s1 = s_ref[0, hh, lt]
        s2 = s_ref[1, hh, lt]
        _, vals1 = _top16(s1, False)
        rank2, vals2 = _top16(s2, True)
        v1a = jnp.zeros((SUBLANES, LANES), F32)
        v1b = jnp.zeros((SUBLANES, LANES), F32)
        for r in range(SUBLANES):
            v1a = jnp.where(row8 == float(r), vals1[r], v1a)
            v1b = jnp.where(row8 == float(r), vals1[r + SUBLANES], v1b)
        lists = [v1a + vals2[r] for r in range(PEER_TOPK)]
        hb = v1b + vals2[0]
        na = jnp.zeros((SUBLANES, LANES), F32)
        nb = jnp.zeros((SUBLANES, LANES), F32)
        zsum = None
        top = None
        for t in range(PEER_TOPK):
            head = lists[0]
            m = jnp.max(jnp.maximum(head, hb), axis=0, keepdims=True)
            ia = jnp.where(head == m, row8, 99.0)
            ib = jnp.where(hb == m, row8 + float(SUBLANES), 99.0)
            first = jnp.min(jnp.minimum(ia, ib), axis=0, keepdims=True)
            popa = row8 == first
            popb = (row8 + float(SUBLANES)) == first
            if t == 0:
                top = m
                zsum = jnp.ones_like(m)
            else:
                zsum = zsum + jnp.exp(m - top)
            na = na + jnp.where(popa, 1.0, 0.0)
            nb = nb + jnp.where(popb, 1.0, 0.0)
            lists = [jnp.where(popa, lists[k + 1], lists[k]) for k in range(PEER_TOPK - 1 - t)]
            hb = jnp.where(popb, NEG_INF, hb)
        n1 = jnp.zeros((PEER_NKEYS, LANES), F32)
        for r in range(PEER_TOPK):
            nr = na[r:r + 1] if r < SUBLANES else nb[r - SUBLANES:r - SUBLANES + 1]
            n1 = jnp.where(s1 == vals1[r], nr, n1)
        ch = lt // (TOK_CHUNK // LANES)
        lo = pl.ds(pl.multiple_of((lt % (TOK_CHUNK // LANES)) * LANES, LANES), LANES)
        rank2_ref[hh, ch, :, lo] = rank2.astype(BF16)
        e2_ref[hh, ch, :, lo] = (jnp.exp(s2 - vals2[0]) * (1.0 / zsum)).astype(BF16)
        n1_ref[hh, ch, :, lo] = n1
        e1_ref[hh, ch, :, lo] = jnp.exp(s1 - vals1[0])
        return carry

    lax.fori_loop(0, PEER_HEADS * nlt, sel, 0)


def _route_call(ymix, x0, mod3, w_out, ln1_g, ln1_b, w_q, k1, k2, ts):
    B, S, D = x0.shape
    T = B * S
    nS = S // ts
    nlt = ts // LANES
    dq = w_q.shape[1]
    full = lambda shape: pl.BlockSpec(shape, lambda b, s: (0,) * len(shape))
    nch = ts // TOK_CHUNK
    tab = lambda dt: jax.ShapeDtypeStruct((PEER_HEADS, T // TOK_CHUNK, PEER_NKEYS, TOK_CHUNK), dt)
    tab_spec = pl.BlockSpec((PEER_HEADS, nch, PEER_NKEYS, TOK_CHUNK), lambda b, s: (0, b * nS + s, 0, 0))
    return pl.pallas_call(
        functools.partial(_route_kernel, ts=ts),
        grid=(B, nS),
        in_specs=[pl.BlockSpec((1, ts, D), lambda b, s: (b, s, 0)),
                  pl.BlockSpec((1, ts, D), lambda b, s: (b, s, 0)),
                  pl.BlockSpec((1, 6, D), lambda b, s: (b, 0, 0)),
                  full((D, D)), full((1, D)), full((1, D)), full((D, dq)),
                  full((PEER_NKEYS, PEER_DKEY // 2)), full((PEER_NKEYS, PEER_DKEY // 2))],
        out_specs=[pl.BlockSpec((1, ts, D), lambda b, s: (b, s, 0)),
                   pl.BlockSpec((D, ts), lambda b, s: (0, b * nS + s)),
                   tab_spec, tab_spec, tab_spec, tab_spec],
        out_shape=[jax.ShapeDtypeStruct((B, S, D), F32),
                   jax.ShapeDtypeStruct((D, T), BF16),
                   tab(BF16), tab(BF16), tab(F32), tab(F32)],
        scratch_shapes=[pltpu.VMEM((2, PEER_HEADS, nlt, PEER_NKEYS, LANES), F32)],
        compiler_params=pltpu.CompilerParams(
            dimension_semantics=("arbitrary", "arbitrary"), vmem_limit_bytes=VMEM_LIMIT),
        name="route",
    )(ymix, x0, mod3, w_out, ln1_g, ln1_b, w_q, k1, k2)


def _peer_kernel(h2t_ref, u_ref, vt_ref, rank2_ref, e2_ref, n1_ref, e1_ref, x1_ref, mod_ref,
                 g2_ref, b2_ref, o_ref, acc_ref, p_ref, act_ref, rank2_s, e2_s, *, tm, te):
    j = pl.program_id(1)
    nct = tm // TOK_CHUNK
    nblk = te // PEER_NKEYS

    @pl.when(j == 0)
    def _():
        acc_ref[...] = jnp.zeros_like(acc_ref)
        rank2_s[...] = rank2_ref[...]
        e2_s[...] = e2_ref[...]

    grp_rows = BLK_GROUP * PEER_NKEYS

    def act_group(b0):
        rs = slice(b0 * PEER_NKEYS, b0 * PEER_NKEYS + grp_rows)
        act_ref[rs, :] = jnp.dot(u_ref[rs, :], h2t_ref[...], preferred_element_type=F32)

    def key_row(ref, hh, c, b):
        halves = [jnp.broadcast_to(ref[hh, c, b:b + 1, l0:l0 + LANES], (BF16_ROWS, LANES)).astype(BF16)
                  for l0 in range(0, TOK_CHUNK, LANES)]
        return jnp.concatenate(halves, axis=1)

    zero = jnp.zeros((BF16_ROWS, TOK_CHUNK), BF16)
    act_group(0)
    for b0 in range(0, nblk, BLK_GROUP):
        blks = range(b0, b0 + BLK_GROUP)
        if b0 + BLK_GROUP < nblk:
            act_group(b0 + BLK_GROUP)
        for c in range(nct):
            cs = slice(c * TOK_CHUNK, (c + 1) * TOK_CHUNK)
            for k0 in range(0, NPIECE, PIECE_GROUP):
                pieces = range(k0, k0 + PIECE_GROUP)
                w = {(b, k): zero for b in blks for k in pieces}
                for hh in range(PEER_HEADS):
                    n1v = {b: key_row(n1_ref, hh, c, b) for b in blks}
                    e1v = {b: key_row(e1_ref, hh, c, b) for b in blks}
                    for k in pieces:
                        ks = slice(k * BF16_ROWS, (k + 1) * BF16_ROWS)
                        r2 = rank2_s[hh, c, ks, :]
                        e2 = e2_s[hh, c, ks, :]
                        for b in blks:
                            w[b, k] = w[b, k] + jnp.where(r2 < n1v[b], e2, zero) * e1v[b]
                for b in blks:
                    r0 = b * PEER_NKEYS + k0 * BF16_ROWS
                    rs = slice(r0, r0 + PIECE_GROUP * BF16_ROWS)
                    wb = jnp.concatenate([w[b, k] for k in pieces], axis=0)
                    p_ref[rs, cs] = wb * _gelu(act_ref[rs, cs]).astype(BF16)
    acc_ref[...] += jnp.dot(vt_ref[...], p_ref[...], preferred_element_type=F32)

    @pl.when(j == pl.num_programs(1) - 1)
    def _():
        mod = mod_ref[0]
        y = acc_ref[...].T
        o_ref[...] = _layer_norm(ALPHA * x1_ref[...] + mod[5:6] * y, g2_ref[...], b2_ref[...])


def _peer_call(h2t, u_b, vt_b, rank2, e2, n1, e1, x1f, mod3, ln2_g, ln2_b, S, tm, te):
    D, T = h2t.shape
    ne = u_b.shape[0]
    nct = tm // TOK_CHUNK
    nblk = te // PEER_NKEYS
    tiles_per_batch = S // tm
    full = lambda shape: pl.BlockSpec(shape, lambda i, j: (0,) * len(shape))
    tok_tab = pl.BlockSpec((PEER_HEADS, nct, PEER_NKEYS, TOK_CHUNK), lambda i, j: (0, i, 0, 0))
    key_tab = pl.BlockSpec((PEER_HEADS, nct, nblk, TOK_CHUNK), lambda i, j: (0, i, j, 0))
    return pl.pallas_call(
        functools.partial(_peer_kernel, tm=tm, te=te),
        grid=(T // tm, ne // te),
        in_specs=[pl.BlockSpec((D, tm), lambda i, j: (0, i)),
                  pl.BlockSpec((te, D), lambda i, j: (j, 0)),
                  pl.BlockSpec((D, te), lambda i, j: (0, j)),
                  tok_tab, tok_tab, key_tab, key_tab,
                  pl.BlockSpec((tm, D), lambda i, j: (i, 0)),
                  pl.BlockSpec((1, 6, D), lambda i, j: (i // tiles_per_batch, 0, 0)),
                  full((1, D)), full((1, D))],
        out_specs=pl.BlockSpec((tm, D), lambda i, j: (i, 0)),
        out_shape=jax.ShapeDtypeStruct((T, D), F32),
        scratch_shapes=[pltpu.VMEM((D, tm), F32),
                        pltpu.VMEM((te, tm), BF16),
                        pltpu.VMEM((te, tm), F32),
                        pltpu.VMEM((PEER_HEADS, nct, PEER_NKEYS, TOK_CHUNK), BF16),
                        pltpu.VMEM((PEER_HEADS, nct, PEER_NKEYS, TOK_CHUNK), BF16)],
        compiler_params=pltpu.CompilerParams(
            dimension_semantics=("arbitrary", "arbitrary"), vmem_limit_bytes=VMEM_LIMIT),
        name="peer",
    )(h2t, u_b, vt_b, rank2, e2, n1, e1, x1f, mod3, ln2_g, ln2_b)


def _block_diag(w):
    nh, hd, _ = w.shape
    eye = jnp.eye(nh, dtype=w.dtype)
    return (eye[:, None, :, None] * w[:, :, None, :]).reshape(nh * hd, nh * hd)


def kernel(x, c, ln0_g, ln0_b, ada_w, ada_b, w_in, pool_w, pool_scale, conv_w, conv_b, lru_wa, lru_ba, lru_wx, lru_bx, lru_lambda, w_out, ln1_g, ln1_b, peer_wq, peer_k1, peer_k2, peer_u, peer_v, ln2_g, ln2_b, *, ts=512, tm=512, te=1024):
    B, S, D = x.shape
    T = B * S
    row = lambda v: v.reshape(1, -1)
    l = 0
    c_pad = jnp.pad(c, ((0, SUBLANES - B), (0, 0)))
    mod = _mod_call(c_pad, ada_w[l], row(ada_b[l]))[:B]
    mod3 = mod.reshape(B, 6, D)

    wg = jnp.concatenate([_block_diag(lru_wa[l]), _block_diag(lru_wx[l])], axis=1).astype(BF16)
    bg = jnp.concatenate([lru_ba[l].reshape(1, -1), lru_bx[l].reshape(1, -1)], axis=1)
    x0, ymix = _mixer_call(x, mod3, row(ln0_g), row(ln0_b), w_in[l].astype(BF16),
                           pool_w[l].astype(BF16), row(pool_scale[l]),
                           conv_w[l].reshape(CONV_W, D_LRU), row(conv_b[l]), wg, bg,
                           row(lru_lambda[l]), ts)
    x1, h2t, rank2, e2, n1, e1 = _route_call(
        ymix, x0, mod3, w_out[l].astype(BF16), row(ln1_g[l]), row(ln1_b[l]),
        peer_wq[l].astype(BF16), peer_k1[l].astype(BF16), peer_k2[l].astype(BF16), ts)
    out = _peer_call(h2t, peer_u[l].astype(BF16), peer_v[l].T.astype(BF16), rank2, e2, n1, e1,
                     x1.reshape(T, D), mod3, row(ln2_g[l]), row(ln2_b[l]), S, tm, te)
    return out.reshape(B, S, D)
```

```python
import functools

import jax
import jax.numpy as jnp
from jax import lax
from jax.experimental import pallas as pl
from jax.experimental.pallas import tpu as pltpu

F32 = jnp.float32
BF16 = jnp.bfloat16

D_MODEL = 1024
D_POOL = 512
D_LRU = 512
POOL_WINDOWS = (2, 4, 8, 16)
POOL_GW = 128
POOL_HALO = 16
LRU_HEADS = 8
LRU_HD = 64
CONV_W = 4
CONV_HALO = 8
LRU_C = 8.0
PEER_HEADS = 8
PEER_NKEYS = 128
PEER_DKEY = 256
PEER_TOPK = 16
PEER_N = PEER_NKEYS * PEER_NKEYS
LN_EPS = 1e-5
DEPTH = 1
ALPHA = (2.0 * DEPTH) ** 0.25

LANES = 128
SUBLANES = 8
BF16_ROWS = 2 * SUBLANES
NPIECE = PEER_NKEYS // BF16_ROWS
TOK_CHUNK = 2 * LANES
BLK_GROUP = 2
VMEM_LIMIT = 56 * 1024 * 1024

NEG_INF = float("-inf")


def _layer_norm(x, g, b):
    mu = jnp.mean(x, axis=-1, keepdims=True)
    xc = x - mu
    var = jnp.mean(xc * xc, axis=-1, keepdims=True)
    return xc * lax.rsqrt(var + LN_EPS) * g + b


_GELU_A = -2.0 * 0.7978845608028654 * 1.4426950408889634
_GELU_B = _GELU_A * 0.044715


def _gelu(x):
    return x / (1.0 + jnp.exp2(x * (_GELU_A + _GELU_B * (x * x))))


def _sigmoid(x):
    return 1.0 / (1.0 + jnp.exp(-x))


def _mod_kernel(c_ref, w_ref, b_ref, o_ref):
    c = c_ref[...]
    cond = c * _sigmoid(c)
    o_ref[...] = jnp.dot(cond, w_ref[...], preferred_element_type=F32) + b_ref[...]


def _mod_call(c_pad, ada_w, ada_b, tn=1024):
    rows, d = c_pad.shape
    n = ada_w.shape[1]
    return pl.pallas_call(
        _mod_kernel,
        grid=(n // tn,),
        in_specs=[pl.BlockSpec((rows, d), lambda j: (0, 0)),
                  pl.BlockSpec((d, tn), lambda j: (0, j)),
                  pl.BlockSpec((1, tn), lambda j: (0, j))],
        out_specs=pl.BlockSpec((rows, tn), lambda j: (0, j)),
        out_shape=jax.ShapeDtypeStruct((rows, n), F32),
        name="mod",
    )(c_pad, ada_w, ada_b)


def _mixer_kernel(x_ref, mod_ref, g0_ref, b0_ref, win_ref, pw_ref, ps_ref, cw_ref, cb_ref,
                  wg_ref, bg_ref, lam_ref, x0_ref, ym_ref,
                  zp_ext, zx_ext, a_s, b_s, h_s, hc_ref, *, ts):
    s = pl.program_id(1)
    x0 = _layer_norm(x_ref[0], g0_ref[...], b0_ref[...])
    x0_ref[0] = x0
    mod = mod_ref[0]
    h = x0 * (1.0 + mod[1:2]) + mod[0:1]
    z = jnp.dot(h.astype(BF16), win_ref[...], preferred_element_type=F32)

    @pl.when(s == 0)
    def _():
        zp_ext[0:POOL_HALO, :] = jnp.zeros((POOL_HALO, D_POOL), F32)
        zx_ext[0:CONV_HALO, :] = jnp.zeros((CONV_HALO, D_LRU), F32)
        hc_ref[...] = jnp.zeros_like(hc_ref)

    @pl.when(s > 0)
    def _():
        zp_ext[0:POOL_HALO, :] = zp_ext[ts:ts + POOL_HALO, :]
        zx_ext[0:CONV_HALO, :] = zx_ext[ts:ts + CONV_HALO, :]

    zp_ext[POOL_HALO:, :] = z[:, :D_POOL]
    zx_ext[CONV_HALO:, :] = z[:, D_POOL:D_POOL + D_LRU]
    zg = z[:, D_POOL + D_LRU:]

    t_glob = lax.broadcasted_iota(jnp.int32, (ts, POOL_GW), 0) + s * ts
    outs = []
    for g, w in enumerate(POOL_WINDOWS):
        cs = slice(g * POOL_GW, (g + 1) * POOL_GW)
        cur = zp_ext[POOL_HALO:POOL_HALO + ts, cs]
        acc = cur
        for k in range(1, w):
            acc = acc + zp_ext[POOL_HALO - k:POOL_HALO - k + ts, cs]
        cnt = jnp.minimum(t_glob + 1, w).astype(F32)
        y = acc / cnt - cur
        outs.append(jnp.dot(y.astype(BF16), pw_ref[g], preferred_element_type=F32))
    y_pool = jnp.concatenate(outs, axis=1) * ps_ref[...]

    cw = cw_ref[...]
    xc = cb_ref[...] + cw[0:1] * zx_ext[CONV_HALO - 3:CONV_HALO - 3 + ts, :]
    for k in range(1, CONV_W):
        xc = xc + cw[k:k + 1] * zx_ext[CONV_HALO - 3 + k:CONV_HALO - 3 + k + ts, :]
    gates = jnp.dot(xc.astype(BF16), wg_ref[...], preferred_element_type=F32) + bg_ref[...]
    r = _sigmoid(gates[:, :D_LRU])
    i = _sigmoid(gates[:, D_LRU:])
    nl = -lam_ref[...]
    softplus = jnp.maximum(nl, 0.0) + jnp.log(1.0 + jnp.exp(-jnp.abs(nl)))
    a = jnp.exp((-LRU_C * softplus) * r)
    bv = jnp.sqrt(jnp.maximum(1.0 - a * a, 0.0)) * (i * xc)

    row_in_grp = lax.broadcasted_iota(jnp.int32, (ts, D_LRU), 0) % SUBLANES
    for k in (1, 2, 4):
        a_sh = pltpu.roll(a, k, 0)
        b_sh = pltpu.roll(bv, k, 0)
        valid = row_in_grp >= k
        bv = jnp.where(valid, a * b_sh + bv, bv)
        a = jnp.where(valid, a * a_sh, a)
    a_s[...] = a
    b_s[...] = bv

    def grp(gi, hprev):
        off = pl.multiple_of(gi * SUBLANES, SUBLANES)
        hg = a_s[pl.ds(off, SUBLANES), :] * hprev + b_s[pl.ds(off, SUBLANES), :]
        h_s[pl.ds(off, SUBLANES), :] = hg
        return hg[SUBLANES - 1:SUBLANES, :]

    hc_ref[0:1, :] = lax.fori_loop(0, ts // SUBLANES, grp, hc_ref[0:1, :])
    y_lru = h_s[...] * _gelu(zg)
    ym_ref[0] = jnp.concatenate([y_pool, y_lru], axis=1).astype(BF16)


def _mixer_call(x, mod3, ln0_g, ln0_b, w_in, pool_w, pool_scale, conv_w, conv_b, wg, bg, lam, ts):
    B, S, D = x.shape
    dz = w_in.shape[1]
    full = lambda shape: pl.BlockSpec(shape, lambda b, s: (0,) * len(shape))
    return pl.pallas_call(
        functools.partial(_mixer_kernel, ts=ts),
        grid=(B, S // ts),
        in_specs=[pl.BlockSpec((1, ts, D), lambda b, s: (b, s, 0)),
                  pl.BlockSpec((1, 6, D), lambda b, s: (b, 0, 0)),
                  full((1, D)), full((1, D)), full((D, dz)),
                  full((len(POOL_WINDOWS), POOL_GW, POOL_GW)), full((1, D_POOL)),
                  full((CONV_W, D_LRU)), full((1, D_LRU)),
                  full((D_LRU, 2 * D_LRU)), full((1, 2 * D_LRU)), full((1, D_LRU))],
        out_specs=[pl.BlockSpec((1, ts, D), lambda b, s: (b, s, 0)),
                   pl.BlockSpec((1, ts, D), lambda b, s: (b, s, 0))],
        out_shape=[jax.ShapeDtypeStruct((B, S, D), F32),
                   jax.ShapeDtypeStruct((B, S, D), BF16)],
        scratch_shapes=[pltpu.VMEM((POOL_HALO + ts, D_POOL), F32),
                        pltpu.VMEM((CONV_HALO + ts, D_LRU), F32),
                        pltpu.VMEM((ts, D_LRU), F32),
                        pltpu.VMEM((ts, D_LRU), F32),
                        pltpu.VMEM((ts, D_LRU), F32),
                        pltpu.VMEM((SUBLANES, D_LRU), F32)],
        compiler_params=pltpu.CompilerParams(
            dimension_semantics=("arbitrary", "arbitrary"), vmem_limit_bytes=VMEM_LIMIT),
        name="mixer",
    )(x, mod3, ln0_g, ln0_b, w_in, pool_w, pool_scale, conv_w, conv_b, wg, bg, lam)


def _top16(s, with_rank):
    work = s
    rank = jnp.full(s.shape, float(PEER_TOPK), F32) if with_rank else None
    vals = []
    for r in range(PEER_TOPK):
        m = jnp.max(work, axis=0, keepdims=True)
        eq = work == m
        if with_rank:
            rank = jnp.where(eq, float(r), rank)
        work = jnp.where(eq, NEG_INF, work)
        vals.append(m)
    return rank, vals


def _route_kernel(ym_ref, x0_ref, mod_ref, wout_ref, g1_ref, b1_ref, wq_ref, k1_ref, k2_ref,
                  x1_ref, h2t_ref, rank2_ref, e2_ref, n1_ref, e1_ref, s_ref, *, ts):
    nlt = ts // LANES
    mod = mod_ref[0]
    y = jnp.dot(ym_ref[0], wout_ref[...], preferred_element_type=F32)
    x1 = _layer_norm(ALPHA * x0_ref[0] + mod[2:3] * y, g1_ref[...], b1_ref[...])
    x1_ref[0] = x1
    h2 = x1 * (1.0 + mod[4:5]) + mod[3:4]
    h2t_ref[...] = h2.T.astype(BF16)
    q = jnp.dot(h2.astype(BF16), wq_ref[...], preferred_element_type=F32)
    half = PEER_DKEY // 2
    nt = (((1,), (1,)), ((), ()))
    for hh in range(PEER_HEADS):
        qh = q[:, hh * PEER_DKEY:(hh + 1) * PEER_DKEY]
        mu = jnp.mean(qh, axis=-1, keepdims=True)
        qc = qh - mu
        var = jnp.mean(qc * qc, axis=-1, keepdims=True)
        qb = (qc * lax.rsqrt(var + LN_EPS)).astype(BF16)
        s1t = lax.dot_general(k1_ref[...], qb[:, :half], nt, preferred_element_type=F32)
        s2t = lax.dot_general(k2_ref[...], qb[:, half:], nt, preferred_element_type=F32)
        for lt in range(nlt):
            s_ref[0, hh, lt] = s1t[:, lt * LANES:(lt + 1) * LANES]
            s_ref[1, hh, lt] = s2t[:, lt * LANES:(lt + 1) * LANES]

    row8 = lax.broadcasted_iota(jnp.int32, (SUBLANES, LANES), 0).astype(F32)

    nunit = PEER_HEADS * nlt

    def unit_slot(u):
        hh = u // nlt
        lt = u % nlt
        ch = lt // (TOK_CHUNK // LANES)
        lo = pl.ds(pl.multiple_of((lt % (TOK_CHUNK // LANES)) * LANES, LANES), LANES)
        return hh, lt, ch, lo

    def rows16(va, vb, r):
        return va[r:r + 1] if r < SUBLANES else vb[r - SUBLANES:r - SUBLANES + 1]

    def sel(idx, carry):
        v1a, v1b, v2a, v2b = carry

        hh, lt, ch, lo = unit_slot(jnp.minimum(idx, nunit - 1))
        s1 = s_ref[0, hh, lt]
        s2 = s_ref[1, hh, lt]
        _, vals1 = _top16(s1, False)
        rank2, vals2 = _top16(s2, True)
        rank2_ref[hh, ch, :, lo] = rank2.astype(BF16)
        e1_ref[hh, ch, :, lo] = jnp.exp(s1 - vals1[0])
        nxt = [jnp.zeros((SUBLANES, LANES), F32) for _ in range(4)]
        for r in range(SUBLANES):
            pick = row8 == float(r)
            nxt[0] = jnp.where(pick, vals1[r], nxt[0])
            nxt[1] = jnp.where(pick, vals1[r + SUBLANES], nxt[1])
            nxt[2] = jnp.where(pick, vals2[r], nxt[2])
            nxt[3] = jnp.where(pick, vals2[r + SUBLANES], nxt[3])

        hp, ltp, chp, lop = unit_slot(jnp.maximum(idx - 1, 0))
        lists = [v1a + rows16(v2a, v2b, r) for r in range(PEER_TOPK)]
        hb = v1b + v2a[0:1]
        na = jnp.zeros((SUBLANES, LANES), F32)
        nb = jnp.zeros((SUBLANES, LANES), F32)
        zsum = None
        top = None
        for t in range(PEER_TOPK):
            head = lists[0]
            m = jnp.max(jnp.maximum(head, hb), axis=0, keepdims=True)
            ia = jnp.where(head == m, row8, 99.0)
            ib = jnp.where(hb == m, row8 + float(SUBLANES), 99.0)
            first = jnp.min(jnp.minimum(ia, ib), axis=0, keepdims=True)
            popa = row8 == first
            popb = (row8 + float(SUBLANES)) == first
            if t == 0:
                top = m
                zsum = jnp.ones_like(m)
            else:
                zsum = zsum + jnp.exp(m - top)
            na = na + jnp.where(popa, 1.0, 0.0)
            nb = nb + jnp.where(popb, 1.0, 0.0)
            lists = [jnp.where(popa, lists[k + 1], lists[k]) for k in range(PEER_TOPK - 1 - t)]
            hb = jnp.where(popb, NEG_INF, hb)
        s1p = s_ref[0, hp, ltp]
        s2p = s_ref[1, hp, ltp]
        n1 = jnp.zeros((PEER_NKEYS, LANES), F32)
        for r in range(PEER_TOPK):
            n1 = jnp.where(s1p == rows16(v1a, v1b, r), rows16(na, nb, r), n1)
        n1_ref[hp, chp, :, lop] = n1
        e2_ref[hp, chp, :, lop] = (jnp.exp(s2p - v2a[0:1]) * (1.0 / zsum)).astype(BF16)
        return tuple(nxt)

    zeros8 = jnp.zeros((SUBLANES, LANES), F32)
    lax.fori_loop(0, nunit + 1, sel, (zeros8, zeros8, zeros8, zeros8))


def _route_call(ymix, x0, mod3, w_out, ln1_g, ln1_b, w_q, k1, k2, ts):
    B, S, D = x0.shape
    T = B * S
    nS = S // ts
    nlt = ts // LANES
    dq = w_q.shape[1]
    full = lambda shape: pl.BlockSpec(shape, lambda b, s: (0,) * len(shape))
    nch = ts // TOK_CHUNK
    tab = lambda dt: jax.ShapeDtypeStruct((PEER_HEADS, T // TOK_CHUNK, PEER_NKEYS, TOK_CHUNK), dt)
    tab_spec = pl.BlockSpec((PEER_HEADS, nch, PEER_NKEYS, TOK_CHUNK), lambda b, s: (0, b * nS + s, 0, 0))
    return pl.pallas_call(
        functools.partial(_route_kernel, ts=ts),
        grid=(B, nS),
        in_specs=[pl.BlockSpec((1, ts, D), lambda b, s: (b, s, 0)),
                  pl.BlockSpec((1, ts, D), lambda b, s: (b, s, 0)),
                  pl.BlockSpec((1, 6, D), lambda b, s: (b, 0, 0)),
                  full((D, D)), full((1, D)), full((1, D)), full((D, dq)),
                  full((PEER_NKEYS, PEER_DKEY // 2)), full((PEER_NKEYS, PEER_DKEY // 2))],
        out_specs=[pl.BlockSpec((1, ts, D), lambda b, s: (b, s, 0)),
                   pl.BlockSpec((D, ts), lambda b, s: (0, b * nS + s)),
                   tab_spec, tab_spec, tab_spec, tab_spec],
        out_shape=[jax.ShapeDtypeStruct((B, S, D), F32),
                   jax.ShapeDtypeStruct((D, T), BF16),
                   tab(BF16), tab(BF16), tab(F32), tab(F32)],
        scratch_shapes=[pltpu.VMEM((2, PEER_HEADS, nlt, PEER_NKEYS, LANES), F32)],
        compiler_params=pltpu.CompilerParams(
            dimension_semantics=("arbitrary", "arbitrary"), vmem_limit_bytes=VMEM_LIMIT),
        name="route",
    )(ymix, x0, mod3, w_out, ln1_g, ln1_b, w_q, k1, k2)


def _peer_kernel(h2t_ref, u_ref, vt_ref, rank2_ref, e2_ref, n1_ref, e1_ref, x1_ref, mod_ref,
                 g2_ref, b2_ref, o_ref, acc_ref, p_ref, act_ref, rank2_s, e2_s, *, tm, te):
    j = pl.program_id(1)
    nct = tm // TOK_CHUNK
    nblk = te // PEER_NKEYS

    @pl.when(j == 0)
    def _():
        acc_ref[...] = jnp.zeros_like(acc_ref)
        rank2_s[...] = rank2_ref[...]
        e2_s[...] = e2_ref[...]

    grp_rows = BLK_GROUP * PEER_NKEYS

    def act_group(b0):
        rs = slice(b0 * PEER_NKEYS, b0 * PEER_NKEYS + grp_rows)
        act_ref[rs, :] = jnp.dot(u_ref[rs, :], h2t_ref[...], preferred_element_type=F32)

    def key_row(ref, hh, c, b):
        halves = [jnp.broadcast_to(ref[hh, c, b:b + 1, l0:l0 + LANES], (BF16_ROWS, LANES)).astype(BF16)
                  for l0 in range(0, TOK_CHUNK, LANES)]
        return jnp.concatenate(halves, axis=1)

    zero = jnp.zeros((BF16_ROWS, TOK_CHUNK), BF16)
    act_group(0)
    for b0 in range(0, nblk, BLK_GROUP):
        blks = range(b0, b0 + BLK_GROUP)
        if b0 + BLK_GROUP < nblk:
            act_group(b0 + BLK_GROUP)
        for c in range(nct):
            cs = slice(c * TOK_CHUNK, (c + 1) * TOK_CHUNK)
            w = {(b, k): zero for b in blks for k in range(NPIECE)}
            for hh in range(PEER_HEADS):
                n1v = {b: key_row(n1_ref, hh, c, b) for b in blks}
                e1v = {b: key_row(e1_ref, hh, c, b) for b in blks}
                for k in range(NPIECE):
                    ks = slice(k * BF16_ROWS, (k + 1) * BF16_ROWS)
                    r2 = rank2_s[hh, c, ks, :]
                    e2 = e2_s[hh, c, ks, :]
                    for b in blks:
                        w[b, k] = w[b, k] + jnp.where(r2 < n1v[b], e2, zero) * e1v[b]
            for b in blks:
                rs = slice(b * PEER_NKEYS, (b + 1) * PEER_NKEYS)
                wb = jnp.concatenate([w[b, k] for k in range(NPIECE)], axis=0)
                p_ref[rs, cs] = wb * _gelu(act_ref[rs, cs]).astype(BF16)
    acc_ref[...] += jnp.dot(vt_ref[...], p_ref[...], preferred_element_type=F32)

    @pl.when(j == pl.num_programs(1) - 1)
    def _():
        mod = mod_ref[0]
        y = acc_ref[...].T
        o_ref[...] = _layer_norm(ALPHA * x1_ref[...] + mod[5:6] * y, g2_ref[...], b2_ref[...])


def _peer_call(h2t, u_b, vt_b, rank2, e2, n1, e1, x1f, mod3, ln2_g, ln2_b, S, tm, te):
    D, T = h2t.shape
    ne = u_b.shape[0]
    nct = tm // TOK_CHUNK
    nblk = te // PEER_NKEYS
    tiles_per_batch = S // tm
    full = lambda shape: pl.BlockSpec(shape, lambda i, j: (0,) * len(shape))
    tok_tab = pl.BlockSpec((PEER_HEADS, nct, PEER_NKEYS, TOK_CHUNK), lambda i, j: (0, i, 0, 0))
    key_tab = pl.BlockSpec((PEER_HEADS, nct, nblk, TOK_CHUNK), lambda i, j: (0, i, j, 0))
    return pl.pallas_call(
        functools.partial(_peer_kernel, tm=tm, te=te),
        grid=(T // tm, ne // te),
        in_specs=[pl.BlockSpec((D, tm), lambda i, j: (0, i)),
                  pl.BlockSpec((te, D), lambda i, j: (j, 0)),
                  pl.BlockSpec((D, te), lambda i, j: (0, j)),
                  tok_tab, tok_tab, key_tab, key_tab,
                  pl.BlockSpec((tm, D), lambda i, j: (i, 0)),
                  pl.BlockSpec((1, 6, D), lambda i, j: (i // tiles_per_batch, 0, 0)),
                  full((1, D)), full((1, D))],
        out_specs=pl.BlockSpec((tm, D), lambda i, j: (i, 0)),
        out_shape=jax.ShapeDtypeStruct((T, D), F32),
        scratch_shapes=[pltpu.VMEM((D, tm), F32),
                        pltpu.VMEM((te, tm), BF16),
                        pltpu.VMEM((te, tm), F32),
                        pltpu.VMEM((PEER_HEADS, nct, PEER_NKEYS, TOK_CHUNK), BF16),
                        pltpu.VMEM((PEER_HEADS, nct, PEER_NKEYS, TOK_CHUNK), BF16)],
        compiler_params=pltpu.CompilerParams(
            dimension_semantics=("arbitrary", "arbitrary"), vmem_limit_bytes=VMEM_LIMIT),
        name="peer",
    )(h2t, u_b, vt_b, rank2, e2, n1, e1, x1f, mod3, ln2_g, ln2_b)


def _block_diag(w):
    nh, hd, _ = w.shape
    eye = jnp.eye(nh, dtype=w.dtype)
    return (eye[:, None, :, None] * w[:, :, None, :]).reshape(nh * hd, nh * hd)


def kernel(x, c, ln0_g, ln0_b, ada_w, ada_b, w_in, pool_w, pool_scale, conv_w, conv_b, lru_wa, lru_ba, lru_wx, lru_bx, lru_lambda, w_out, ln1_g, ln1_b, peer_wq, peer_k1, peer_k2, peer_u, peer_v, ln2_g, ln2_b, *, ts=512, tm=512, te=2048):
    B, S, D = x.shape
    T = B * S
    row = lambda v: v.reshape(1, -1)
    l = 0
    c_pad = jnp.pad(c, ((0, SUBLANES - B), (0, 0)))
    mod = _mod_call(c_pad, ada_w[l], row(ada_b[l]))[:B]
    mod3 = mod.reshape(B, 6, D)

    wg = jnp.concatenate([_block_diag(lru_wa[l]), _block_diag(lru_wx[l])], axis=1).astype(BF16)
    bg = jnp.concatenate([lru_ba[l].reshape(1, -1), lru_bx[l].reshape(1, -1)], axis=1)
    x0, ymix = _mixer_call(x, mod3, row(ln0_g), row(ln0_b), w_in[l].astype(BF16),
                           pool_w[l].astype(BF16), row(pool_scale[l]),
                           conv_w[l].reshape(CONV_W, D_LRU), row(conv_b[l]), wg, bg,
                           row(lru_lambda[l]), ts)
    x1, h2t, rank2, e2, n1, e1 = _route_call(
        ymix, x0, mod3, w_out[l].astype(BF16), row(ln1_g[l]), row(ln1_b[l]),
        peer_wq[l].astype(BF16), peer_k1[l].astype(BF16), peer_k2[l].astype(BF16), ts)
    out = _peer_call(h2t, peer_u[l].astype(BF16), peer_v[l].T.astype(BF16), rank2, e2, n1, e1,
                     x1.reshape(T, D), mod3, row(ln2_g[l]), row(ln2_b[l]), S, tm, te)
    return out.reshape(B, S, D)
```

```python
import functools

import jax
import jax.numpy as jnp
from jax import lax
from jax.experimental import pallas as pl
from jax.experimental.pallas import tpu as pltpu

F32 = jnp.float32
BF16 = jnp.bfloat16

D_MODEL = 1024
D_POOL = 512
D_LRU = 512
POOL_WINDOWS = (2, 4, 8, 16)
POOL_GW = 128
POOL_HALO = 16
LRU_HEADS = 8
LRU_HD = 64
CONV_W = 4
CONV_HALO = 8
LRU_C = 8.0
PEER_HEADS = 8
PEER_NKEYS = 128
PEER_DKEY = 256
PEER_TOPK = 16
PEER_N = PEER_NKEYS * PEER_NKEYS
LN_EPS = 1e-5
DEPTH = 1
ALPHA = (2.0 * DEPTH) ** 0.25

LANES = 128
SUBLANES = 8
BF16_ROWS = 2 * SUBLANES
NPIECE = PEER_NKEYS // BF16_ROWS
TOK_CHUNK = 2 * LANES
SEL_UNITS = 1
BLK_GROUP = 2
VMEM_LIMIT = 56 * 1024 * 1024
NEG_INF = float("-inf")


def _layer_norm(x, g, b):
    mu = jnp.mean(x, axis=-1, keepdims=True)
    xc = x - mu
    var = jnp.mean(xc * xc, axis=-1, keepdims=True)
    return xc * lax.rsqrt(var + LN_EPS) * g + b


_GELU_A = -2.0 * 0.7978845608028654 * 1.4426950408889634
_GELU_B = _GELU_A * 0.044715


def _gelu(x):
    return x / (1.0 + jnp.exp2(x * (_GELU_A + _GELU_B * (x * x))))


def _sigmoid(x):
    return 1.0 / (1.0 + jnp.exp(-x))


def _mod_kernel(c_ref, w_ref, b_ref, o_ref):
    c = c_ref[...]
    cond = c * _sigmoid(c)
    o_ref[...] = jnp.dot(cond, w_ref[...], preferred_element_type=F32) + b_ref[...]


def _mod_call(c_pad, ada_w, ada_b, tn=1024):
    rows, d = c_pad.shape
    n = ada_w.shape[1]
    return pl.pallas_call(
        _mod_kernel,
        grid=(n // tn,),
        in_specs=[pl.BlockSpec((rows, d), lambda j: (0, 0)),
                  pl.BlockSpec((d, tn), lambda j: (0, j)),
                  pl.BlockSpec((1, tn), lambda j: (0, j))],
        out_specs=pl.BlockSpec((rows, tn), lambda j: (0, j)),
        out_shape=jax.ShapeDtypeStruct((rows, n), F32),
        name="mod",
    )(c_pad, ada_w, ada_b)


def _mixer_kernel(x_ref, mod_ref, g0_ref, b0_ref, win_ref, pw_ref, ps_ref, cw_ref, cb_ref,
                  wg_ref, bg_ref, lam_ref, x0_ref, ym_ref,
                  zp_ext, zx_ext, a_s, b_s, h_s, hc_ref, *, ts):
    s = pl.program_id(1)
    x0 = _layer_norm(x_ref[0], g0_ref[...], b0_ref[...])
    x0_ref[0] = x0
    mod = mod_ref[0]
    h = x0 * (1.0 + mod[1:2]) + mod[0:1]
    z = jnp.dot(h.astype(BF16), win_ref[...], preferred_element_type=F32)

    @pl.when(s == 0)
    def _():
        zp_ext[0:POOL_HALO, :] = jnp.zeros((POOL_HALO, D_POOL), F32)
        zx_ext[0:CONV_HALO, :] = jnp.zeros((CONV_HALO, D_LRU), F32)
        hc_ref[...] = jnp.zeros_like(hc_ref)

    @pl.when(s > 0)
    def _():
        zp_ext[0:POOL_HALO, :] = zp_ext[ts:ts + POOL_HALO, :]
        zx_ext[0:CONV_HALO, :] = zx_ext[ts:ts + CONV_HALO, :]

    zp_ext[POOL_HALO:, :] = z[:, :D_POOL]
    zx_ext[CONV_HALO:, :] = z[:, D_POOL:D_POOL + D_LRU]
    zg = z[:, D_POOL + D_LRU:]

    t_glob = lax.broadcasted_iota(jnp.int32, (ts, POOL_GW), 0) + s * ts
    outs = []
    for g, w in enumerate(POOL_WINDOWS):
        cs = slice(g * POOL_GW, (g + 1) * POOL_GW)
        cur = zp_ext[POOL_HALO:POOL_HALO + ts, cs]
        acc = cur
        for k in range(1, w):
            acc = acc + zp_ext[POOL_HALO - k:POOL_HALO - k + ts, cs]
        cnt = jnp.minimum(t_glob + 1, w).astype(F32)
        y = acc / cnt - cur
        outs.append(jnp.dot(y.astype(BF16), pw_ref[g], preferred_element_type=F32))
    y_pool = jnp.concatenate(outs, axis=1) * ps_ref[...]

    cw = cw_ref[...]
    xc = cb_ref[...] + cw[0:1] * zx_ext[CONV_HALO - 3:CONV_HALO - 3 + ts, :]
    for k in range(1, CONV_W):
        xc = xc + cw[k:k + 1] * zx_ext[CONV_HALO - 3 + k:CONV_HALO - 3 + k + ts, :]
    gates = jnp.dot(xc.astype(BF16), wg_ref[...], preferred_element_type=F32) + bg_ref[...]
    r = _sigmoid(gates[:, :D_LRU])
    i = _sigmoid(gates[:, D_LRU:])
    nl = -lam_ref[...]
    softplus = jnp.maximum(nl, 0.0) + jnp.log(1.0 + jnp.exp(-jnp.abs(nl)))
    a = jnp.exp((-LRU_C * softplus) * r)
    om = 1.0 - a * a
    bv = jnp.where(om > 0.0, om * lax.rsqrt(om), 0.0) * (i * xc)

    a = a.reshape(ts // SUBLANES, SUBLANES, D_LRU)
    bv = bv.reshape(ts // SUBLANES, SUBLANES, D_LRU)
    row_in_grp = lax.broadcasted_iota(jnp.int32, a.shape, 1)
    for k in (1, 2, 4):
        a_sh = pltpu.roll(a, k, 1)
        b_sh = pltpu.roll(bv, k, 1)
        valid = row_in_grp >= k
        bv = jnp.where(valid, a * b_sh + bv, bv)
        a = jnp.where(valid, a * a_sh, a)
    a_s[...] = a.reshape(ts, D_LRU)
    b_s[...] = bv.reshape(ts, D_LRU)

    def grp(gi, hprev):
        off = pl.multiple_of(gi * SUBLANES, SUBLANES)
        hg = a_s[pl.ds(off, SUBLANES), :] * hprev + b_s[pl.ds(off, SUBLANES), :]
        h_s[pl.ds(off, SUBLANES), :] = hg
        return hg[SUBLANES - 1:SUBLANES, :]

    hc_ref[0:1, :] = lax.fori_loop(0, ts // SUBLANES, grp, hc_ref[0:1, :])
    y_lru = h_s[...] * _gelu(zg)
    ym_ref[0] = jnp.concatenate([y_pool, y_lru], axis=1).astype(BF16)


def _mixer_call(x, mod3, ln0_g, ln0_b, w_in, pool_w, pool_scale, conv_w, conv_b, wg, bg, lam, ts):
    B, S, D = x.shape
    dz = w_in.shape[1]
    full = lambda shape: pl.BlockSpec(shape, lambda b, s: (0,) * len(shape))
    return pl.pallas_call(
        functools.partial(_mixer_kernel, ts=ts),
        grid=(B, S // ts),
        in_specs=[pl.BlockSpec((1, ts, D), lambda b, s: (b, s, 0)),
                  pl.BlockSpec((1, 6, D), lambda b, s: (b, 0, 0)),
                  full((1, D)), full((1, D)), full((D, dz)),
                  full((len(POOL_WINDOWS), POOL_GW, POOL_GW)), full((1, D_POOL)),
                  full((CONV_W, D_LRU)), full((1, D_LRU)),
                  full((D_LRU, 2 * D_LRU)), full((1, 2 * D_LRU)), full((1, D_LRU))],
        out_specs=[pl.BlockSpec((1, ts, D), lambda b, s: (b, s, 0)),
                   pl.BlockSpec((1, ts, D), lambda b, s: (b, s, 0))],
        out_shape=[jax.ShapeDtypeStruct((B, S, D), F32),
                   jax.ShapeDtypeStruct((B, S, D), BF16)],
        scratch_shapes=[pltpu.VMEM((POOL_HALO + ts, D_POOL), F32),
                        pltpu.VMEM((CONV_HALO + ts, D_LRU), F32),
                        pltpu.VMEM((ts, D_LRU), F32),
                        pltpu.VMEM((ts, D_LRU), F32),
                        pltpu.VMEM((ts, D_LRU), F32),
                        pltpu.VMEM((SUBLANES, D_LRU), F32)],
        compiler_params=pltpu.CompilerParams(
            dimension_semantics=("arbitrary", "arbitrary"), vmem_limit_bytes=VMEM_LIMIT),
        name="mixer",
    )(x, mod3, ln0_g, ln0_b, w_in, pool_w, pool_scale, conv_w, conv_b, wg, bg, lam)


def _oddeven_merge_sort_pairs(n):
    pairs = []
    p = 1
    while p < n:
        k = p
        while k >= 1:
            for j in range(k % p, n - k, 2 * k):
                for i in range(min(k, n - j - k)):
                    if (i + j) // (2 * p) == (i + j + k) // (2 * p):
                        pairs.append((i + j, i + j + k))
            k //= 2
        p *= 2
    return pairs


_SORT16 = _oddeven_merge_sort_pairs(PEER_TOPK)


def _top16_sorted(rows):
    v = list(rows)
    for i, j in _SORT16:
        v[i], v[j] = jnp.maximum(v[i], v[j]), jnp.minimum(v[i], v[j])
    shift = SUBLANES // 2
    while shift >= 1:
        other = [pltpu.roll(x, shift, 0) for x in v]
        v = [jnp.maximum(v[k], other[PEER_TOPK - 1 - k]) for k in range(PEER_TOPK)]
        d = PEER_TOPK // 2
        while d >= 1:
            for i in range(PEER_TOPK):
                if i & d == 0:
                    v[i], v[i + d] = jnp.maximum(v[i], v[i + d]), jnp.minimum(v[i], v[i + d])
            d //= 2
        shift //= 2
    return v


def _route_kernel(ym_ref, x0_ref, mod_ref, wout_ref, g1_ref, b1_ref, wq_ref, k1_ref, k2_ref,
                  x1_ref, h2t_ref, rank2_ref, e2_ref, n1_ref, e1_ref, s_ref, *, ts):
    nlt = ts // LANES
    mod = mod_ref[0]
    y = jnp.dot(ym_ref[0], wout_ref[...], preferred_element_type=F32)
    x1 = _layer_norm(ALPHA * x0_ref[0] + mod[2:3] * y, g1_ref[...], b1_ref[...])
    x1_ref[0] = x1
    h2 = x1 * (1.0 + mod[4:5]) + mod[3:4]
    h2t_ref[...] = h2.T.astype(BF16)
    q = jnp.dot(h2.astype(BF16), wq_ref[...], preferred_element_type=F32)
    half = PEER_DKEY // 2
    nt = (((1,), (1,)), ((), ()))
    for hh in range(PEER_HEADS):
        qh = q[:, hh * PEER_DKEY:(hh + 1) * PEER_DKEY]
        mu = jnp.mean(qh, axis=-1, keepdims=True)
        qc = qh - mu
        var = jnp.mean(qc * qc, axis=-1, keepdims=True)
        qb = (qc * lax.rsqrt(var + LN_EPS)).astype(BF16)
        s1t = lax.dot_general(k1_ref[...], qb[:, :half], nt, preferred_element_type=F32)
        s2t = lax.dot_general(k2_ref[...], qb[:, half:], nt, preferred_element_type=F32)
        for lt in range(nlt):
            s_ref[0, hh, lt] = s1t[:, lt * LANES:(lt + 1) * LANES]
            s_ref[1, hh, lt] = s2t[:, lt * LANES:(lt + 1) * LANES]

    row8 = lax.broadcasted_iota(jnp.int32, (SUBLANES, LANES), 0).astype(F32)

    nunit = PEER_HEADS * nlt

    def unit_slot(u):
        hh = u // nlt
        lt = u % nlt
        ch = lt // (TOK_CHUNK // LANES)
        lo = pl.ds(pl.multiple_of((lt % (TOK_CHUNK // LANES)) * LANES, LANES), LANES)
        return hh, lt, ch, lo

    def rows16(va, vb, r):
        return va[r:r + 1] if r < SUBLANES else vb[r - SUBLANES:r - SUBLANES + 1]

    def one(u_ext, u_mrg, carry):
        v1a, v1b, v2a, v2b = carry

        hh, lt, ch, lo = unit_slot(u_ext)
        s1 = s_ref[0, hh, lt]
        s2 = s_ref[1, hh, lt]
        nrow = PEER_NKEYS // SUBLANES
        rows2 = [s2[k * SUBLANES:(k + 1) * SUBLANES] for k in range(nrow)]
        vals1 = _top16_sorted([s1[k * SUBLANES:(k + 1) * SUBLANES] for k in range(nrow)])
        vals2 = _top16_sorted(rows2)
        rank2 = []
        for row in rows2:
            cnt = jnp.where(vals2[0] > row, 1.0, 0.0)
            for r in range(1, PEER_TOPK):
                cnt = cnt + jnp.where(vals2[r] > row, 1.0, 0.0)
            rank2.append(cnt)
        rank2_ref[hh, ch, :, lo] = jnp.concatenate(rank2, axis=0).astype(BF16)
        e1_ref[hh, ch, :, lo] = jnp.exp(s1 - vals1[0][0:1])
        nxt = [jnp.zeros((SUBLANES, LANES), F32) for _ in range(4)]
        for r in range(SUBLANES):
            pick = row8 == float(r)
            nxt[0] = jnp.where(pick, vals1[r], nxt[0])
            nxt[1] = jnp.where(pick, vals1[r + SUBLANES], nxt[1])
            nxt[2] = jnp.where(pick, vals2[r], nxt[2])
            nxt[3] = jnp.where(pick, vals2[r + SUBLANES], nxt[3])

        hp, ltp, chp, lop = unit_slot(u_mrg)
        lists = [v1a + rows16(v2a, v2b, r) for r in range(PEER_TOPK)]
        hb = v1b + v2a[0:1]
        na = jnp.zeros((SUBLANES, LANES), F32)
        nb = jnp.zeros((SUBLANES, LANES), F32)
        zsum = None
        top = None
        for t in range(PEER_TOPK):
            head = lists[0]
            m = jnp.max(jnp.maximum(head, hb), axis=0, keepdims=True)
            ia = jnp.where(head == m, row8, 99.0)
            ib = jnp.where(hb == m, row8 + float(SUBLANES), 99.0)
            first = jnp.min(jnp.minimum(ia, ib), axis=0, keepdims=True)
            popa = row8 == first
            popb = (row8 + float(SUBLANES)) == first
            if t == 0:
                top = m
                zsum = jnp.ones_like(m)
            else:
                zsum = zsum + jnp.exp(m - top)
            na = na + jnp.where(popa, 1.0, 0.0)
            nb = nb + jnp.where(popb, 1.0, 0.0)
            lists = [jnp.where(popa, lists[k + 1], lists[k]) for k in range(PEER_TOPK - 1 - t)]
            hb = jnp.where(popb, NEG_INF, hb)
        s1p = s_ref[0, hp, ltp]
        s2p = s_ref[1, hp, ltp]
        n1 = jnp.zeros((PEER_NKEYS, LANES), F32)
        for r in range(PEER_TOPK):
            n1 = jnp.where(s1p == rows16(v1a, v1b, r), rows16(na, nb, r), n1)
        n1_ref[hp, chp, :, lop] = n1
        e2_ref[hp, chp, :, lop] = (jnp.exp(s2p - v2a[0:1]) * (1.0 / zsum)).astype(BF16)
        return nxt

    def trip(t, carry):
        out = []
        for q in range(SEL_UNITS):
            u_ext = jnp.minimum(SEL_UNITS * t + q, nunit - SEL_UNITS + q)
            u_mrg = jnp.maximum(SEL_UNITS * (t - 1) + q, q)
            out += one(u_ext, u_mrg, carry[4 * q:4 * q + 4])
        return tuple(out)

    zeros8 = jnp.zeros((SUBLANES, LANES), F32)
    lax.fori_loop(0, nunit // SEL_UNITS + 1, trip, (zeros8,) * (4 * SEL_UNITS))


def _route_call(ymix, x0, mod3, w_out, ln1_g, ln1_b, w_q, k1, k2, ts):
    B, S, D = x0.shape
    T = B * S
    nS = S // ts
    nlt = ts // LANES
    dq = w_q.shape[1]
    full = lambda shape: pl.BlockSpec(shape, lambda b, s: (0,) * len(shape))
    nch = ts // TOK_CHUNK
    tab = lambda dt: jax.ShapeDtypeStruct((PEER_HEADS, T // TOK_CHUNK, PEER_NKEYS, TOK_CHUNK), dt)
    tab_spec = pl.BlockSpec((PEER_HEADS, nch, PEER_NKEYS, TOK_CHUNK), lambda b, s: (0, b * nS + s, 0, 0))
    return pl.pallas_call(
        functools.partial(_route_kernel, ts=ts),
        grid=(B, nS),
        in_specs=[pl.BlockSpec((1, ts, D), lambda b, s: (b, s, 0)),
                  pl.BlockSpec((1, ts, D), lambda b, s: (b, s, 0)),
                  pl.BlockSpec((1, 6, D), lambda b, s: (b, 0, 0)),
                  full((D, D)), full((1, D)), full((1, D)), full((D, dq)),
                  full((PEER_NKEYS, PEER_DKEY // 2)), full((PEER_NKEYS, PEER_DKEY // 2))],
        out_specs=[pl.BlockSpec((1, ts, D), lambda b, s: (b, s, 0)),
                   pl.BlockSpec((D, ts), lambda b, s: (0, b * nS + s)),
                   tab_spec, tab_spec, tab_spec, tab_spec],
        out_shape=[jax.ShapeDtypeStruct((B, S, D), F32),
                   jax.ShapeDtypeStruct((D, T), BF16),
                   tab(BF16), tab(BF16), tab(F32), tab(F32)],
        scratch_shapes=[pltpu.VMEM((2, PEER_HEADS, nlt, PEER_NKEYS, LANES), F32)],
        compiler_params=pltpu.CompilerParams(
            dimension_semantics=("arbitrary", "arbitrary"), vmem_limit_bytes=VMEM_LIMIT),
        name="route",
    )(ymix, x0, mod3, w_out, ln1_g, ln1_b, w_q, k1, k2)


def _peer_kernel(h2t_ref, u_ref, vt_ref, rank2_ref, e2_ref, n1_ref, e1_ref, x1_ref, mod_ref,
                 g2_ref, b2_ref, o_ref, acc_ref, p_ref, act_ref, rank2_s, e2_s, *, tm, te):
    j = pl.program_id(1)
    nct = tm // TOK_CHUNK
    nblk = te // PEER_NKEYS

    @pl.when(j == 0)
    def _():
        acc_ref[...] = jnp.zeros_like(acc_ref)
        rank2_s[...] = rank2_ref[...]
        e2_s[...] = e2_ref[...]

    grp_rows = BLK_GROUP * PEER_NKEYS

    def act_group(b0):
        rs = slice(b0 * PEER_NKEYS, b0 * PEER_NKEYS + grp_rows)
        act_ref[rs, :] = jnp.dot(u_ref[rs, :], h2t_ref[...], preferred_element_type=F32)

    def key_row(ref, hh, c, b):
        halves = [jnp.broadcast_to(ref[hh, c, b:b + 1, l0:l0 + LANES], (BF16_ROWS, LANES)).astype(BF16)
                  for l0 in range(0, TOK_CHUNK, LANES)]
        return jnp.concatenate(halves, axis=1)

    zero = jnp.zeros((BF16_ROWS, TOK_CHUNK), BF16)

    act_group(0)
    for b0 in range(0, nblk, BLK_GROUP):
        blks = range(b0, b0 + BLK_GROUP)
        if b0 + BLK_GROUP < nblk:
            act_group(b0 + BLK_GROUP)
        for c in range(nct):
            cs = slice(c * TOK_CHUNK, (c + 1) * TOK_CHUNK)
            w = {(b, k): zero for b in blks for k in range(NPIECE)}
            for hh in range(PEER_HEADS):
                n1v = {b: key_row(n1_ref, hh, c, b) for b in blks}
                e1v = {b: key_row(e1_ref, hh, c, b) for b in blks}
                for k in range(NPIECE):
                    ks = slice(k * BF16_ROWS, (k + 1) * BF16_ROWS)
                    r2 = rank2_s[hh, c, ks, :]
                    e2 = e2_s[hh, c, ks, :]
                    for b in blks:
                        w[b, k] = w[b, k] + jnp.where(r2 < n1v[b], e2, zero) * e1v[b]
            for b in blks:
                rs = slice(b * PEER_NKEYS, (b + 1) * PEER_NKEYS)
                wb = jnp.concatenate([w[b, k] for k in range(NPIECE)], axis=0)
                p_ref[rs, cs] = wb * _gelu(act_ref[rs, cs]).astype(BF16)
    acc_ref[...] += jnp.dot(vt_ref[...], p_ref[...], preferred_element_type=F32)

    @pl.when(j == pl.num_programs(1) - 1)
    def _():
        mod = mod_ref[0]
        y = acc_ref[...].T
        o_ref[...] = _layer_norm(ALPHA * x1_ref[...] + mod[5:6] * y, g2_ref[...], b2_ref[...])


def _peer_call(h2t, u_b, vt_b, rank2, e2, n1, e1, x1f, mod3, ln2_g, ln2_b, S, tm, te):
    D, T = h2t.shape
    ne = u_b.shape[0]
    nct = tm // TOK_CHUNK
    nblk = te // PEER_NKEYS
    tiles_per_batch = S // tm
    full = lambda shape: pl.BlockSpec(shape, lambda i, j: (0,) * len(shape))
    tok_tab = pl.BlockSpec((PEER_HEADS, nct, PEER_NKEYS, TOK_CHUNK), lambda i, j: (0, i, 0, 0))
    key_tab = pl.BlockSpec((PEER_HEADS, nct, nblk, TOK_CHUNK), lambda i, j: (0, i, j, 0))
    return pl.pallas_call(
        functools.partial(_peer_kernel, tm=tm, te=te),
        grid=(T // tm, ne // te),
        in_specs=[pl.BlockSpec((D, tm), lambda i, j: (0, i)),
                  pl.BlockSpec((te, D), lambda i, j: (j, 0)),
                  pl.BlockSpec((D, te), lambda i, j: (0, j)),
                  tok_tab, tok_tab, key_tab, key_tab,
                  pl.BlockSpec((tm, D), lambda i, j: (i, 0)),
                  pl.BlockSpec((1, 6, D), lambda i, j: (i // tiles_per_batch, 0, 0)),
                  full((1, D)), full((1, D))],
        out_specs=pl.BlockSpec((tm, D), lambda i, j: (i, 0)),
        out_shape=jax.ShapeDtypeStruct((T, D), F32),
        scratch_shapes=[pltpu.VMEM((D, tm), F32),
                        pltpu.VMEM((te, tm), BF16),
                        pltpu.VMEM((te, tm), F32),
                        pltpu.VMEM((PEER_HEADS, nct, PEER_NKEYS, TOK_CHUNK), BF16),
                        pltpu.VMEM((PEER_HEADS, nct, PEER_NKEYS, TOK_CHUNK), BF16)],
        compiler_params=pltpu.CompilerParams(
            dimension_semantics=("arbitrary", "arbitrary"), vmem_limit_bytes=VMEM_LIMIT),
        name="peer",
    )(h2t, u_b, vt_b, rank2, e2, n1, e1, x1f, mod3, ln2_g, ln2_b)


def _block_diag(w):
    nh, hd, _ = w.shape
    eye = jnp.eye(nh, dtype=w.dtype)
    return (eye[:, None, :, None] * w[:, :, None, :]).reshape(nh * hd, nh * hd)


def kernel(x, c, ln0_g, ln0_b, ada_w, ada_b, w_in, pool_w, pool_scale, conv_w, conv_b, lru_wa, lru_ba, lru_wx, lru_bx, lru_lambda, w_out, ln1_g, ln1_b, peer_wq, peer_k1, peer_k2, peer_u, peer_v, ln2_g, ln2_b, *, ts=512, tm=512, te=2048):
    B, S, D = x.shape
    T = B * S
    row = lambda v: v.reshape(1, -1)
    l = 0
    c_pad = jnp.pad(c, ((0, SUBLANES - B), (0, 0)))
    mod = _mod_call(c_pad, ada_w[l], row(ada_b[l]))[:B]
    mod3 = mod.reshape(B, 6, D)

    wg = jnp.concatenate([_block_diag(lru_wa[l]), _block_diag(lru_wx[l])], axis=1).astype(BF16)
    bg = jnp.concatenate([lru_ba[l].reshape(1, -1), lru_bx[l].reshape(1, -1)], axis=1)
    x0, ymix = _mixer_call(x, mod3, row(ln0_g), row(ln0_b), w_in[l].astype(BF16),
                           pool_w[l].astype(BF16), row(pool_scale[l]),
                           conv_w[l].reshape(CONV_W, D_LRU), row(conv_b[l]), wg, bg,
                           row(lru_lambda[l]), ts)
    x1, h2t, rank2, e2, n1, e1 = _route_call(
        ymix, x0, mod3, w_out[l].astype(BF16), row(ln1_g[l]), row(ln1_b[l]),
        peer_wq[l].astype(BF16), peer_k1[l].astype(BF16), peer_k2[l].astype(BF16), ts)
    out = _peer_call(h2t, peer_u[l].astype(BF16), peer_v[l].T.astype(BF16), rank2, e2, n1, e1,
                     x1.reshape(T, D), mod3, row(ln2_g[l]), row(ln2_b[l]), S, tm, te)
    return out.reshape(B, S, D)
```

```python
import functools

import jax
import jax.numpy as jnp
from jax import lax
from jax.experimental import pallas as pl
from jax.experimental.pallas import tpu as pltpu

F32 = jnp.float32
BF16 = jnp.bfloat16

D_MODEL = 1024
D_POOL = 512
D_LRU = 512
POOL_WINDOWS = (2, 4, 8, 16)
POOL_GW = 128
POOL_HALO = 16
LRU_HEADS = 8
LRU_HD = 64
CONV_W = 4
CONV_HALO = 8
LRU_C = 8.0
PEER_HEADS = 8
PEER_NKEYS = 128
PEER_DKEY = 256
PEER_TOPK = 16
PEER_N = PEER_NKEYS * PEER_NKEYS
LN_EPS = 1e-5
DEPTH = 1
ALPHA = (2.0 * DEPTH) ** 0.25

LANES = 128
SUBLANES = 8
BF16_ROWS = 2 * SUBLANES
NPIECE = PEER_NKEYS // BF16_ROWS
TOK_CHUNK = 2 * LANES
BLK_GROUP = 2
VMEM_LIMIT = 56 * 1024 * 1024
NEG_INF = float("-inf")


def _layer_norm(x, g, b):
    mu = jnp.mean(x, axis=-1, keepdims=True)
    xc = x - mu
    var = jnp.mean(xc * xc, axis=-1, keepdims=True)
    return xc * lax.rsqrt(var + LN_EPS) * g + b


_GELU_A = -2.0 * 0.7978845608028654 * 1.4426950408889634
_GELU_B = _GELU_A * 0.044715


def _gelu(x):
    return x / (1.0 + jnp.exp2(x * (_GELU_A + _GELU_B * (x * x))))


def _sigmoid(x):
    return 1.0 / (1.0 + jnp.exp(-x))


def _mod_kernel(c_ref, w_ref, b_ref, o_ref):
    c = c_ref[...]
    cond = c * _sigmoid(c)
    o_ref[...] = jnp.dot(cond, w_ref[...], preferred_element_type=F32) + b_ref[...]


def _mod_call(c_pad, ada_w, ada_b, tn=1024):
    rows, d = c_pad.shape
    n = ada_w.shape[1]
    return pl.pallas_call(
        _mod_kernel,
        grid=(n // tn,),
        in_specs=[pl.BlockSpec((rows, d), lambda j: (0, 0)),
                  pl.BlockSpec((d, tn), lambda j: (0, j)),
                  pl.BlockSpec((1, tn), lambda j: (0, j))],
        out_specs=pl.BlockSpec((rows, tn), lambda j: (0, j)),
        out_shape=jax.ShapeDtypeStruct((rows, n), F32),
        name="mod",
    )(c_pad, ada_w, ada_b)


def _mixer_kernel(x_ref, mod_ref, g0_ref, b0_ref, win_ref, pw_ref, ps_ref, cw_ref, cb_ref,
                  wg_ref, bg_ref, lam_ref, x0_ref, ym_ref,
                  zp_ext, zx_ext, a_s, b_s, h_s, hc_ref, *, ts):
    s = pl.program_id(1)
    x0 = _layer_norm(x_ref[0], g0_ref[...], b0_ref[...])
    x0_ref[0] = x0
    mod = mod_ref[0]
    h = x0 * (1.0 + mod[1:2]) + mod[0:1]
    z = jnp.dot(h.astype(BF16), win_ref[...], preferred_element_type=F32)

    @pl.when(s == 0)
    def _():
        zp_ext[0:POOL_HALO, :] = jnp.zeros((POOL_HALO, D_POOL), F32)
        zx_ext[0:CONV_HALO, :] = jnp.zeros((CONV_HALO, D_LRU), F32)
        hc_ref[...] = jnp.zeros_like(hc_ref)

    @pl.when(s > 0)
    def _():
        zp_ext[0:POOL_HALO, :] = zp_ext[ts:ts + POOL_HALO, :]
        zx_ext[0:CONV_HALO, :] = zx_ext[ts:ts + CONV_HALO, :]

    zp_ext[POOL_HALO:, :] = z[:, :D_POOL]
    zx_ext[CONV_HALO:, :] = z[:, D_POOL:D_POOL + D_LRU]
    zg = z[:, D_POOL + D_LRU:]

    t_glob = lax.broadcasted_iota(jnp.int32, (ts, POOL_GW), 0) + s * ts
    outs = []
    for g, w in enumerate(POOL_WINDOWS):
        cs = slice(g * POOL_GW, (g + 1) * POOL_GW)
        cur = zp_ext[POOL_HALO:POOL_HALO + ts, cs]
        acc = cur
        for k in range(1, w):
            acc = acc + zp_ext[POOL_HALO - k:POOL_HALO - k + ts, cs]
        cnt = jnp.minimum(t_glob + 1, w).astype(F32)
        y = acc / cnt - cur
        outs.append(jnp.dot(y.astype(BF16), pw_ref[g], preferred_element_type=F32))
    y_pool = jnp.concatenate(outs, axis=1) * ps_ref[...]

    cw = cw_ref[...]
    xc = cb_ref[...] + cw[0:1] * zx_ext[CONV_HALO - 3:CONV_HALO - 3 + ts, :]
    for k in range(1, CONV_W):
        xc = xc + cw[k:k + 1] * zx_ext[CONV_HALO - 3 + k:CONV_HALO - 3 + k + ts, :]
    gates = jnp.dot(xc.astype(BF16), wg_ref[...], preferred_element_type=F32) + bg_ref[...]
    r = _sigmoid(gates[:, :D_LRU])
    i = _sigmoid(gates[:, D_LRU:])
    nl = -lam_ref[...]
    softplus = jnp.maximum(nl, 0.0) + jnp.log(1.0 + jnp.exp(-jnp.abs(nl)))
    a = jnp.exp((-LRU_C * softplus) * r)
    om = 1.0 - a * a
    bv = jnp.where(om > 0.0, om * lax.rsqrt(om), 0.0) * (i * xc)

    a = a.reshape(ts // SUBLANES, SUBLANES, D_LRU)
    bv = bv.reshape(ts // SUBLANES, SUBLANES, D_LRU)
    row_in_grp = lax.broadcasted_iota(jnp.int32, a.shape, 1)
    for k in (1, 2, 4):
        a_sh = pltpu.roll(a, k, 1)
        b_sh = pltpu.roll(bv, k, 1)
        valid = row_in_grp >= k
        bv = jnp.where(valid, a * b_sh + bv, bv)
        a = jnp.where(valid, a * a_sh, a)
    a_s[...] = a.reshape(ts, D_LRU)
    b_s[...] = bv.reshape(ts, D_LRU)

    def grp(gi, hprev):
        off = pl.multiple_of(gi * SUBLANES, SUBLANES)
        hg = a_s[pl.ds(off, SUBLANES), :] * hprev + b_s[pl.ds(off, SUBLANES), :]
        h_s[pl.ds(off, SUBLANES), :] = hg
        return hg[SUBLANES - 1:SUBLANES, :]

    hc_ref[0:1, :] = lax.fori_loop(0, ts // SUBLANES, grp, hc_ref[0:1, :])
    y_lru = h_s[...] * _gelu(zg)
    ym_ref[0] = jnp.concatenate([y_pool, y_lru], axis=1).astype(BF16)


def _mixer_call(x, mod3, ln0_g, ln0_b, w_in, pool_w, pool_scale, conv_w, conv_b, wg, bg, lam, ts):
    B, S, D = x.shape
    dz = w_in.shape[1]
    full = lambda shape: pl.BlockSpec(shape, lambda b, s: (0,) * len(shape))
    return pl.pallas_call(
        functools.partial(_mixer_kernel, ts=ts),
        grid=(B, S // ts),
        in_specs=[pl.BlockSpec((1, ts, D), lambda b, s: (b, s, 0)),
                  pl.BlockSpec((1, 6, D), lambda b, s: (b, 0, 0)),
                  full((1, D)), full((1, D)), full((D, dz)),
                  full((len(POOL_WINDOWS), POOL_GW, POOL_GW)), full((1, D_POOL)),
                  full((CONV_W, D_LRU)), full((1, D_LRU)),
                  full((D_LRU, 2 * D_LRU)), full((1, 2 * D_LRU)), full((1, D_LRU))],
        out_specs=[pl.BlockSpec((1, ts, D), lambda b, s: (b, s, 0)),
                   pl.BlockSpec((1, ts, D), lambda b, s: (b, s, 0))],
        out_shape=[jax.ShapeDtypeStruct((B, S, D), F32),
                   jax.ShapeDtypeStruct((B, S, D), BF16)],
        scratch_shapes=[pltpu.VMEM((POOL_HALO + ts, D_POOL), F32),
                        pltpu.VMEM((CONV_HALO + ts, D_LRU), F32),
                        pltpu.VMEM((ts, D_LRU), F32),
                        pltpu.VMEM((ts, D_LRU), F32),
                        pltpu.VMEM((ts, D_LRU), F32),
                        pltpu.VMEM((SUBLANES, D_LRU), F32)],
        compiler_params=pltpu.CompilerParams(
            dimension_semantics=("arbitrary", "arbitrary"), vmem_limit_bytes=VMEM_LIMIT),
        name="mixer",
    )(x, mod3, ln0_g, ln0_b, w_in, pool_w, pool_scale, conv_w, conv_b, wg, bg, lam)


def _oddeven_merge_sort_pairs(n):
    pairs = []
    p = 1
    while p < n:
        k = p
        while k >= 1:
            for j in range(k % p, n - k, 2 * k):
                for i in range(min(k, n - j - k)):
                    if (i + j) // (2 * p) == (i + j + k) // (2 * p):
                        pairs.append((i + j, i + j + k))
            k //= 2
        p *= 2
    return pairs


_SORT16 = _oddeven_merge_sort_pairs(PEER_TOPK)


def _top16_sorted(rows):
    v = list(rows)
    for i, j in _SORT16:
        v[i], v[j] = jnp.maximum(v[i], v[j]), jnp.minimum(v[i], v[j])
    shift = SUBLANES // 2
    while shift >= 1:
        other = [pltpu.roll(x, shift, 0) for x in v]
        v = [jnp.maximum(v[k], other[PEER_TOPK - 1 - k]) for k in range(PEER_TOPK)]
        d = PEER_TOPK // 2
        while d >= 1:
            for i in range(PEER_TOPK):
                if i & d == 0:
                    v[i], v[i + d] = jnp.maximum(v[i], v[i + d]), jnp.minimum(v[i], v[i + d])
            d //= 2
        shift //= 2
    return v


def _route_unit(s_ref, tabs, nlt, u_ext, u_mrg, carry, tab_head=None):
    rank2_ref, e2_ref, n1_ref, e1_ref = tabs
    row8 = lax.broadcasted_iota(jnp.int32, (SUBLANES, LANES), 0).astype(F32)
    per_chunk = TOK_CHUNK // LANES

    def unit_slot(u):
        hh = u // nlt
        lt = u % nlt
        if isinstance(u, int):
            lo = slice((lt % per_chunk) * LANES, (lt % per_chunk + 1) * LANES)
        else:
            lo = pl.ds(pl.multiple_of((lt % per_chunk) * LANES, LANES), LANES)
        return hh, lt, (hh if tab_head is None else tab_head), lt // per_chunk, lo

    def rows16(va, vb, r):
        return va[r:r + 1] if r < SUBLANES else vb[r - SUBLANES:r - SUBLANES + 1]

    v1a, v1b, v2a, v2b = carry

    hh, lt, th, ch, lo = unit_slot(u_ext)
    s1 = s_ref[0, hh, lt]
    s2 = s_ref[1, hh, lt]
    nrow = PEER_NKEYS // SUBLANES
    rows2 = [s2[k * SUBLANES:(k + 1) * SUBLANES] for k in range(nrow)]
    vals1 = _top16_sorted([s1[k * SUBLANES:(k + 1) * SUBLANES] for k in range(nrow)])
    vals2 = _top16_sorted(rows2)
    rank2 = []
    for row in rows2:
        cnt = jnp.where(vals2[0] > row, 1.0, 0.0)
        for r in range(1, PEER_TOPK):
            cnt = cnt + jnp.where(vals2[r] > row, 1.0, 0.0)
        rank2.append(cnt)
    rank2_ref[th, ch, :, lo] = jnp.concatenate(rank2, axis=0).astype(BF16)
    e1_ref[th, ch, :, lo] = jnp.exp(s1 - vals1[0][0:1])
    nxt = [jnp.zeros((SUBLANES, LANES), F32) for _ in range(4)]
    for r in range(SUBLANES):
        pick = row8 == float(r)
        nxt[0] = jnp.where(pick, vals1[r], nxt[0])
        nxt[1] = jnp.where(pick, vals1[r + SUBLANES], nxt[1])
        nxt[2] = jnp.where(pick, vals2[r], nxt[2])
        nxt[3] = jnp.where(pick, vals2[r + SUBLANES], nxt[3])

    hp, ltp, thp, chp, lop = unit_slot(u_mrg)
    lists = [v1a + rows16(v2a, v2b, r) for r in range(PEER_TOPK)]
    hb = v1b + v2a[0:1]
    na = jnp.zeros((SUBLANES, LANES), F32)
    nb = jnp.zeros((SUBLANES, LANES), F32)
    zsum = None
    top = None
    for t in range(PEER_TOPK):
        head = lists[0]
        m = jnp.max(jnp.maximum(head, hb), axis=0, keepdims=True)
        ia = jnp.where(head == m, row8, 99.0)
        ib = jnp.where(hb == m, row8 + float(SUBLANES), 99.0)
        first = jnp.min(jnp.minimum(ia, ib), axis=0, keepdims=True)
        popa = row8 == first
        popb = (row8 + float(SUBLANES)) == first
        if t == 0:
            top = m
            zsum = jnp.ones_like(m)
        else:
            zsum = zsum + jnp.exp(m - top)
        na = na + jnp.where(popa, 1.0, 0.0)
        nb = nb + jnp.where(popb, 1.0, 0.0)
        lists = [jnp.where(popa, lists[k + 1], lists[k]) for k in range(PEER_TOPK - 1 - t)]
        hb = jnp.where(popb, NEG_INF, hb)
    s1p = s_ref[0, hp, ltp]
    s2p = s_ref[1, hp, ltp]
    n1 = jnp.zeros((PEER_NKEYS, LANES), F32)
    for r in range(PEER_TOPK):
        n1 = jnp.where(s1p == rows16(v1a, v1b, r), rows16(na, nb, r), n1)
    n1_ref[thp, chp, :, lop] = n1
    e2_ref[thp, chp, :, lop] = (jnp.exp(s2p - v2a[0:1]) * (1.0 / zsum)).astype(BF16)
    return nxt


def _route_kernel(ym_ref, x0_ref, mod_ref, wout_ref, g1_ref, b1_ref, wq_ref, k1_ref, k2_ref,
                  x1_ref, h2t_ref, s_ref, *, ts):
    nlt = ts // LANES
    mod = mod_ref[0]
    y = jnp.dot(ym_ref[0], wout_ref[...], preferred_element_type=F32)
    x1 = _layer_norm(ALPHA * x0_ref[0] + mod[2:3] * y, g1_ref[...], b1_ref[...])
    x1_ref[0] = x1
    h2 = x1 * (1.0 + mod[4:5]) + mod[3:4]
    h2t_ref[...] = h2.T.astype(BF16)
    q = jnp.dot(h2.astype(BF16), wq_ref[...], preferred_element_type=F32)
    half = PEER_DKEY // 2
    nt = (((1,), (1,)), ((), ()))
    for hh in range(PEER_HEADS):
        qh = q[:, hh * PEER_DKEY:(hh + 1) * PEER_DKEY]
        mu = jnp.mean(qh, axis=-1, keepdims=True)
        qc = qh - mu
        var = jnp.mean(qc * qc, axis=-1, keepdims=True)
        qb = (qc * lax.rsqrt(var + LN_EPS)).astype(BF16)
        s1t = lax.dot_general(k1_ref[...], qb[:, :half], nt, preferred_element_type=F32)
        s2t = lax.dot_general(k2_ref[...], qb[:, half:], nt, preferred_element_type=F32)
        for lt in range(nlt):
            s_ref[0, hh, lt] = s1t[:, lt * LANES:(lt + 1) * LANES]
            s_ref[1, hh, lt] = s2t[:, lt * LANES:(lt + 1) * LANES]


def _route_call(ymix, x0, mod3, w_out, ln1_g, ln1_b, w_q, k1, k2, ts):
    B, S, D = x0.shape
    T = B * S
    nS = S // ts
    nlt = ts // LANES
    dq = w_q.shape[1]
    full = lambda shape: pl.BlockSpec(shape, lambda b, s: (0,) * len(shape))
    return pl.pallas_call(
        functools.partial(_route_kernel, ts=ts),
        grid=(B, nS),
        in_specs=[pl.BlockSpec((1, ts, D), lambda b, s: (b, s, 0)),
                  pl.BlockSpec((1, ts, D), lambda b, s: (b, s, 0)),
                  pl.BlockSpec((1, 6, D), lambda b, s: (b, 0, 0)),
                  full((D, D)), full((1, D)), full((1, D)), full((D, dq)),
                  full((PEER_NKEYS, PEER_DKEY // 2)), full((PEER_NKEYS, PEER_DKEY // 2))],
        out_specs=[pl.BlockSpec((1, ts, D), lambda b, s: (b, s, 0)),
                   pl.BlockSpec((D, ts), lambda b, s: (0, b * nS + s)),
                   pl.BlockSpec((2, PEER_HEADS, nlt, PEER_NKEYS, LANES), lambda b, s: (0, 0, b * nS + s, 0, 0))],
        out_shape=[jax.ShapeDtypeStruct((B, S, D), F32),
                   jax.ShapeDtypeStruct((D, T), BF16),
                   jax.ShapeDtypeStruct((2, PEER_HEADS, T // LANES, PEER_NKEYS, LANES), F32)],
        compiler_params=pltpu.CompilerParams(
            dimension_semantics=("arbitrary", "arbitrary"), vmem_limit_bytes=VMEM_LIMIT),
        name="route",
    )(ymix, x0, mod3, w_out, ln1_g, ln1_b, w_q, k1, k2)


def _peer_kernel(h2t_ref, u_ref, vt_ref, s_ref, x1_ref, mod_ref, g2_ref, b2_ref, o_ref,
                 acc_ref, p_ref, act_ref, krow_ref, r2a, e2a, n1a, e1a, r2b, e2b, n1b, e1b, *, tm, te):
    i = pl.program_id(0)
    j = pl.program_id(1)
    nct = tm // TOK_CHUNK
    nlt = tm // LANES
    nblk = te // PEER_NKEYS
    grp_rows = BLK_GROUP * PEER_NKEYS
    zero = jnp.zeros((BF16_ROWS, TOK_CHUNK), BF16)
    zeros8 = jnp.zeros((SUBLANES, LANES), F32)

    def route(tabs):
        carry = (zeros8,) * 4
        for q in range(nlt + 1):
            carry = _route_unit(s_ref, tabs, nlt, min(q, nlt - 1), max(q - 1, 0), carry, tab_head=j)

    def dense(tabs):
        rank2_t, e2_t, n1_t, e1_t = tabs
        k0 = pl.multiple_of(j * nblk, nblk)
        krow_ref[0] = n1_t[:, :, pl.ds(k0, nblk), :]
        krow_ref[1] = e1_t[:, :, pl.ds(k0, nblk), :]

        def act_group(b0):
            rs = slice(b0 * PEER_NKEYS, b0 * PEER_NKEYS + grp_rows)
            act_ref[rs, :] = jnp.dot(u_ref[rs, :], h2t_ref[...], preferred_element_type=F32)

        def key_row(t, hh, c, b):
            halves = [jnp.broadcast_to(krow_ref[t, hh, c, b:b + 1, l0:l0 + LANES],
                                       (BF16_ROWS, LANES)).astype(BF16)
                      for l0 in range(0, TOK_CHUNK, LANES)]
            return jnp.concatenate(halves, axis=1)

        act_group(0)
        for b0 in range(0, nblk, BLK_GROUP):
            blks = range(b0, b0 + BLK_GROUP)
            if b0 + BLK_GROUP < nblk:
                act_group(b0 + BLK_GROUP)
            for c in range(nct):
                cs = slice(c * TOK_CHUNK, (c + 1) * TOK_CHUNK)
                w = {(b, k): zero for b in blks for k in range(NPIECE)}
                for hh in range(PEER_HEADS):
                    n1v = {b: key_row(0, hh, c, b) for b in blks}
                    e1v = {b: key_row(1, hh, c, b) for b in blks}
                    for k in range(NPIECE):
                        ks = slice(k * BF16_ROWS, (k + 1) * BF16_ROWS)
                        r2 = rank2_t[hh, c, ks, :]
                        e2 = e2_t[hh, c, ks, :]
                        for b in blks:
                            w[b, k] = w[b, k] + jnp.where(r2 < n1v[b], e2, zero) * e1v[b]
                for b in blks:
                    rs = slice(b * PEER_NKEYS, (b + 1) * PEER_NKEYS)
                    wb = jnp.concatenate([w[b, k] for k in range(NPIECE)], axis=0)
                    p_ref[rs, cs] = wb * _gelu(act_ref[rs, cs]).astype(BF16)
        acc_ref[...] += jnp.dot(vt_ref[...], p_ref[...], preferred_element_type=F32)

    def step(tabs_cur, tabs_next):
        @pl.when(j == 0)
        def _():
            acc_ref[...] = jnp.zeros_like(acc_ref)

        route(tabs_next)
        dense(tabs_cur)

        @pl.when(j == pl.num_programs(1) - 1)
        def _():
            mod = mod_ref[0]
            y = acc_ref[...].T
            o_ref[...] = _layer_norm(ALPHA * x1_ref[...] + mod[5:6] * y, g2_ref[...], b2_ref[...])

    tabs_a = (r2a, e2a, n1a, e1a)
    tabs_b = (r2b, e2b, n1b, e1b)

    @pl.when(i == 0)
    def _():
        route(tabs_a)

    @pl.when(jnp.logical_and(i > 0, i % 2 == 1))
    def _():
        step(tabs_a, tabs_b)

    @pl.when(jnp.logical_and(i > 0, i % 2 == 0))
    def _():
        step(tabs_b, tabs_a)


def _peer_call(h2t, u_b, vt_b, scores, x1f, mod3, ln2_g, ln2_b, S, tm, te):
    D, T = h2t.shape
    ne = u_b.shape[0]
    assert ne // te == PEER_HEADS, "one routing head per expert-tile step"
    nct = tm // TOK_CHUNK
    nlt = tm // LANES
    nblk = te // PEER_NKEYS
    ntile = T // tm
    tiles_per_batch = S // tm
    full = lambda shape: pl.BlockSpec(shape, lambda i, j: (0,) * len(shape))
    prev = lambda i: jnp.maximum(i - 1, 0)
    tab = lambda dt: pltpu.VMEM((PEER_HEADS, nct, PEER_NKEYS, TOK_CHUNK), dt)
    return pl.pallas_call(
        functools.partial(_peer_kernel, tm=tm, te=te),
        grid=(ntile + 1, ne // te),
        in_specs=[pl.BlockSpec((D, tm), lambda i, j: (0, prev(i))),
                  pl.BlockSpec((te, D), lambda i, j: (j, 0)),
                  pl.BlockSpec((D, te), lambda i, j: (0, j)),
                  pl.BlockSpec((2, 1, nlt, PEER_NKEYS, LANES),
                               lambda i, j: (0, j, jnp.minimum(i, ntile - 1), 0, 0)),
                  pl.BlockSpec((tm, D), lambda i, j: (prev(i), 0)),
                  pl.BlockSpec((1, 6, D), lambda i, j: (prev(i) // tiles_per_batch, 0, 0)),
                  full((1, D)), full((1, D))],
        out_specs=pl.BlockSpec((tm, D), lambda i, j: (prev(i), 0)),
        out_shape=jax.ShapeDtypeStruct((T, D), F32),
        scratch_shapes=[pltpu.VMEM((D, tm), F32),
                        pltpu.VMEM((te, tm), BF16),
                        pltpu.VMEM((te, tm), F32),
                        pltpu.VMEM((2, PEER_HEADS, nct, nblk, TOK_CHUNK), F32),
                        tab(BF16), tab(BF16), tab(F32), tab(F32),
                        tab(BF16), tab(BF16), tab(F32), tab(F32)],
        compiler_params=pltpu.CompilerParams(
            dimension_semantics=("arbitrary", "arbitrary"), vmem_limit_bytes=VMEM_LIMIT),
        name="peer",
    )(h2t, u_b, vt_b, scores, x1f, mod3, ln2_g, ln2_b)


def _block_diag(w):
    nh, hd, _ = w.shape
    eye = jnp.eye(nh, dtype=w.dtype)
    return (eye[:, None, :, None] * w[:, :, None, :]).reshape(nh * hd, nh * hd)


def kernel(x, c, ln0_g, ln0_b, ada_w, ada_b, w_in, pool_w, pool_scale, conv_w, conv_b, lru_wa, lru_ba, lru_wx, lru_bx, lru_lambda, w_out, ln1_g, ln1_b, peer_wq, peer_k1, peer_k2, peer_u, peer_v, ln2_g, ln2_b, *, ts=512, tm=512, te=2048):
    B, S, D = x.shape
    T = B * S
    row = lambda v: v.reshape(1, -1)
    l = 0
    c_pad = jnp.pad(c, ((0, SUBLANES - B), (0, 0)))
    mod = _mod_call(c_pad, ada_w[l], row(ada_b[l]))[:B]
    mod3 = mod.reshape(B, 6, D)

    wg = jnp.concatenate([_block_diag(lru_wa[l]), _block_diag(lru_wx[l])], axis=1).astype(BF16)
    bg = jnp.concatenate([lru_ba[l].reshape(1, -1), lru_bx[l].reshape(1, -1)], axis=1)
    x0, ymix = _mixer_call(x, mod3, row(ln0_g), row(ln0_b), w_in[l].astype(BF16),
                           pool_w[l].astype(BF16), row(pool_scale[l]),
                           conv_w[l].reshape(CONV_W, D_LRU), row(conv_b[l]), wg, bg,
                           row(lru_lambda[l]), ts)
    x1, h2t, scores = _route_call(
        ymix, x0, mod3, w_out[l].astype(BF16), row(ln1_g[l]), row(ln1_b[l]),
        peer_wq[l].astype(BF16), peer_k1[l].astype(BF16), peer_k2[l].astype(BF16), ts)
    out = _peer_call(h2t, peer_u[l].astype(BF16), peer_v[l].T.astype(BF16), scores,
                     x1.reshape(T, D), mod3, row(ln2_g[l]), row(ln2_b[l]), S, tm, te)
    return out.reshape(B, S, D)
```

```python
import functools

import jax
import jax.numpy as jnp
from jax import lax
from jax.experimental import pallas as pl
from jax.experimental.pallas import tpu as pltpu

F32 = jnp.float32
BF16 = jnp.bfloat16

D_MODEL = 1024
D_POOL = 512
D_LRU = 512
POOL_WINDOWS = (2, 4, 8, 16)
POOL_GW = 128
POOL_HALO = 16
LRU_HEADS = 8
LRU_HD = 64
CONV_W = 4
CONV_HALO = 8
LRU_C = 8.0
PEER_HEADS = 8
PEER_NKEYS = 128
PEER_DKEY = 256
PEER_TOPK = 16
PEER_N = PEER_NKEYS * PEER_NKEYS
LN_EPS = 1e-5
DEPTH = 1
ALPHA = (2.0 * DEPTH) ** 0.25

LANES = 128
SUBLANES = 8
BF16_ROWS = 2 * SUBLANES
NPIECE = PEER_NKEYS // BF16_ROWS
TOK_CHUNK = 2 * LANES
BLK_GROUP = 2
VMEM_LIMIT = 56 * 1024 * 1024
SEQ_TILE = 512
TOK_TILE = 512
EXPERT_TILE = PEER_N // PEER_HEADS
NEG_INF = float("-inf")


def _layer_norm(x, g, b):
    mu = jnp.mean(x, axis=-1, keepdims=True)
    xc = x - mu
    var = jnp.mean(xc * xc, axis=-1, keepdims=True)
    return xc * lax.rsqrt(var + LN_EPS) * g + b


_GELU_A = -2.0 * 0.7978845608028654 * 1.4426950408889634
_GELU_B = _GELU_A * 0.044715


def _gelu(x):
    return x / (1.0 + jnp.exp2(x * (_GELU_A + _GELU_B * (x * x))))


def _sigmoid(x):
    return 1.0 / (1.0 + jnp.exp(-x))


def _mod_kernel(c_ref, w_ref, b_ref, o_ref):
    c = c_ref[...]
    cond = c * _sigmoid(c)
    o_ref[...] = jnp.dot(cond, w_ref[...], preferred_element_type=F32) + b_ref[...]


def _mod_call(c_pad, ada_w, ada_b, tn=1024):
    rows, d = c_pad.shape
    n = ada_w.shape[1]
    return pl.pallas_call(
        _mod_kernel,
        grid=(n // tn,),
        in_specs=[pl.BlockSpec((rows, d), lambda j: (0, 0)),
                  pl.BlockSpec((d, tn), lambda j: (0, j)),
                  pl.BlockSpec((1, tn), lambda j: (0, j))],
        out_specs=pl.BlockSpec((rows, tn), lambda j: (0, j)),
        out_shape=jax.ShapeDtypeStruct((rows, n), F32),
        name="mod",
    )(c_pad, ada_w, ada_b)


def _mixer_kernel(x_ref, mod_ref, g0_ref, b0_ref, win_ref, pw_ref, ps_ref, cw_ref, cb_ref,
                  wg_ref, bg_ref, lam_ref, x0_ref, ym_ref,
                  zp_ext, zx_ext, a_s, b_s, h_s, hc_ref, *, ts):
    s = pl.program_id(1)
    x0 = _layer_norm(x_ref[0], g0_ref[...], b0_ref[...])
    x0_ref[0] = x0
    mod = mod_ref[0]
    h = x0 * (1.0 + mod[1:2]) + mod[0:1]
    z = jnp.dot(h.astype(BF16), win_ref[...], preferred_element_type=F32)

    @pl.when(s == 0)
    def _():
        zp_ext[0:POOL_HALO, :] = jnp.zeros((POOL_HALO, D_POOL), F32)
        zx_ext[0:CONV_HALO, :] = jnp.zeros((CONV_HALO, D_LRU), F32)
        hc_ref[...] = jnp.zeros_like(hc_ref)

    @pl.when(s > 0)
    def _():
        zp_ext[0:POOL_HALO, :] = zp_ext[ts:ts + POOL_HALO, :]
        zx_ext[0:CONV_HALO, :] = zx_ext[ts:ts + CONV_HALO, :]

    zp_ext[POOL_HALO:, :] = z[:, :D_POOL]
    zx_ext[CONV_HALO:, :] = z[:, D_POOL:D_POOL + D_LRU]
    zg = z[:, D_POOL + D_LRU:]

    t_glob = lax.broadcasted_iota(jnp.int32, (ts, POOL_GW), 0) + s * ts
    outs = []
    for g, w in enumerate(POOL_WINDOWS):
        cs = slice(g * POOL_GW, (g + 1) * POOL_GW)
        cur = zp_ext[POOL_HALO:POOL_HALO + ts, cs]
        acc = cur
        for k in range(1, w):
            acc = acc + zp_ext[POOL_HALO - k:POOL_HALO - k + ts, cs]
        cnt = jnp.minimum(t_glob + 1, w).astype(F32)
        y = acc / cnt - cur
        outs.append(jnp.dot(y.astype(BF16), pw_ref[g], preferred_element_type=F32))
    y_pool = jnp.concatenate(outs, axis=1) * ps_ref[...]

    cw = cw_ref[...]
    xc = cb_ref[...] + cw[0:1] * zx_ext[CONV_HALO - 3:CONV_HALO - 3 + ts, :]
    for k in range(1, CONV_W):
        xc = xc + cw[k:k + 1] * zx_ext[CONV_HALO - 3 + k:CONV_HALO - 3 + k + ts, :]
    gates = jnp.dot(xc.astype(BF16), wg_ref[...], preferred_element_type=F32) + bg_ref[...]
    r = _sigmoid(gates[:, :D_LRU])
    i = _sigmoid(gates[:, D_LRU:])
    nl = -lam_ref[...]
    softplus = jnp.maximum(nl, 0.0) + jnp.log(1.0 + jnp.exp(-jnp.abs(nl)))
    a = jnp.exp((-LRU_C * softplus) * r)
    om = 1.0 - a * a
    bv = jnp.where(om > 0.0, om * lax.rsqrt(om), 0.0) * (i * xc)

    a = a.reshape(ts // SUBLANES, SUBLANES, D_LRU)
    bv = bv.reshape(ts // SUBLANES, SUBLANES, D_LRU)
    row_in_grp = lax.broadcasted_iota(jnp.int32, a.shape, 1)
    for k in (1, 2, 4):
        a_sh = pltpu.roll(a, k, 1)
        b_sh = pltpu.roll(bv, k, 1)
        valid = row_in_grp >= k
        bv = jnp.where(valid, a * b_sh + bv, bv)
        a = jnp.where(valid, a * a_sh, a)
    a_s[...] = a.reshape(ts, D_LRU)
    b_s[...] = bv.reshape(ts, D_LRU)

    def grp(gi, hprev):
        off = pl.multiple_of(gi * SUBLANES, SUBLANES)
        hg = a_s[pl.ds(off, SUBLANES), :] * hprev + b_s[pl.ds(off, SUBLANES), :]
        h_s[pl.ds(off, SUBLANES), :] = hg
        return hg[SUBLANES - 1:SUBLANES, :]

    hc_ref[0:1, :] = lax.fori_loop(0, ts // SUBLANES, grp, hc_ref[0:1, :])
    y_lru = h_s[...] * _gelu(zg)
    ym_ref[0] = jnp.concatenate([y_pool, y_lru], axis=1).astype(BF16)


def _mixer_call(x, mod3, ln0_g, ln0_b, w_in, pool_w, pool_scale, conv_w, conv_b, wg, bg, lam, ts):
    B, S, D = x.shape
    dz = w_in.shape[1]
    full = lambda shape: pl.BlockSpec(shape, lambda b, s: (0,) * len(shape))
    return pl.pallas_call(
        functools.partial(_mixer_kernel, ts=ts),
        grid=(B, S // ts),
        in_specs=[pl.BlockSpec((1, ts, D), lambda b, s: (b, s, 0)),
                  pl.BlockSpec((1, 6, D), lambda b, s: (b, 0, 0)),
                  full((1, D)), full((1, D)), full((D, dz)),
                  full((len(POOL_WINDOWS), POOL_GW, POOL_GW)), full((1, D_POOL)),
                  full((CONV_W, D_LRU)), full((1, D_LRU)),
                  full((D_LRU, 2 * D_LRU)), full((1, 2 * D_LRU)), full((1, D_LRU))],
        out_specs=[pl.BlockSpec((1, ts, D), lambda b, s: (b, s, 0)),
                   pl.BlockSpec((1, ts, D), lambda b, s: (b, s, 0))],
        out_shape=[jax.ShapeDtypeStruct((B, S, D), F32),
                   jax.ShapeDtypeStruct((B, S, D), BF16)],
        scratch_shapes=[pltpu.VMEM((POOL_HALO + ts, D_POOL), F32),
                        pltpu.VMEM((CONV_HALO + ts, D_LRU), F32),
                        pltpu.VMEM((ts, D_LRU), F32),
                        pltpu.VMEM((ts, D_LRU), F32),
                        pltpu.VMEM((ts, D_LRU), F32),
                        pltpu.VMEM((SUBLANES, D_LRU), F32)],
        compiler_params=pltpu.CompilerParams(
            dimension_semantics=("arbitrary", "arbitrary"), vmem_limit_bytes=VMEM_LIMIT),
        name="mixer",
    )(x, mod3, ln0_g, ln0_b, w_in, pool_w, pool_scale, conv_w, conv_b, wg, bg, lam)


def _oddeven_merge_sort_pairs(n):
    pairs = []
    p = 1
    while p < n:
        k = p
        while k >= 1:
            for j in range(k % p, n - k, 2 * k):
                for i in range(min(k, n - j - k)):
                    if (i + j) // (2 * p) == (i + j + k) // (2 * p):
                        pairs.append((i + j, i + j + k))
            k //= 2
        p *= 2
    return pairs


_SORT16 = _oddeven_merge_sort_pairs(PEER_TOPK)


def _top16_sorted(rows):
    v = list(rows)
    for i, j in _SORT16:
        v[i], v[j] = jnp.maximum(v[i], v[j]), jnp.minimum(v[i], v[j])
    shift = SUBLANES // 2
    while shift >= 1:
        other = [pltpu.roll(x, shift, 0) for x in v]
        v = [jnp.maximum(v[k], other[PEER_TOPK - 1 - k]) for k in range(PEER_TOPK)]
        d = PEER_TOPK // 2
        while d >= 1:
            for i in range(PEER_TOPK):
                if i & d == 0:
                    v[i], v[i + d] = jnp.maximum(v[i], v[i + d]), jnp.minimum(v[i], v[i + d])
            d //= 2
        shift //= 2
    return v


def _route_unit(s_ref, tabs, nlt, u_ext, u_mrg, carry, tab_head=None):
    rank2_ref, e2_ref, n1_ref, e1_ref = tabs
    row8 = lax.broadcasted_iota(jnp.int32, (SUBLANES, LANES), 0).astype(F32)
    per_chunk = TOK_CHUNK // LANES

    def unit_slot(u):
        hh = u // nlt
        lt = u % nlt
        if isinstance(u, int):
            lo = slice((lt % per_chunk) * LANES, (lt % per_chunk + 1) * LANES)
        else:
            lo = pl.ds(pl.multiple_of((lt % per_chunk) * LANES, LANES), LANES)
        return hh, lt, (hh if tab_head is None else tab_head), lt // per_chunk, lo

    def rows16(va, vb, r):
        return va[r:r + 1] if r < SUBLANES else vb[r - SUBLANES:r - SUBLANES + 1]

    nxt = None
    if u_ext is not None:
        hh, lt, th, ch, lo = unit_slot(u_ext)
        s1 = s_ref[0, hh, lt]
        s2 = s_ref[1, hh, lt]
        nrow = PEER_NKEYS // SUBLANES
        rows2 = [s2[k * SUBLANES:(k + 1) * SUBLANES] for k in range(nrow)]
        vals1 = _top16_sorted([s1[k * SUBLANES:(k + 1) * SUBLANES] for k in range(nrow)])
        vals2 = _top16_sorted(rows2)
        rank2 = []
        for row in rows2:
            cnt = jnp.where(vals2[0] > row, 1.0, 0.0)
            for r in range(1, PEER_TOPK):
                cnt = cnt + jnp.where(vals2[r] > row, 1.0, 0.0)
            rank2.append(cnt)
        rank2_ref[th, ch, :, lo] = jnp.concatenate(rank2, axis=0).astype(BF16)
        e1_ref[th, ch, :, lo] = jnp.exp(s1 - vals1[0][0:1])
        nxt = [jnp.zeros((SUBLANES, LANES), F32) for _ in range(4)]
        for r in range(SUBLANES):
            pick = row8 == float(r)
            nxt[0] = jnp.where(pick, vals1[r], nxt[0])
            nxt[1] = jnp.where(pick, vals1[r + SUBLANES], nxt[1])
            nxt[2] = jnp.where(pick, vals2[r], nxt[2])
            nxt[3] = jnp.where(pick, vals2[r + SUBLANES], nxt[3])
    if u_mrg is None:
        return nxt
    v1a, v1b, v2a, v2b = carry

    hp, ltp, thp, chp, lop = unit_slot(u_mrg)
    lists = [v1a + rows16(v2a, v2b, r) for r in range(PEER_TOPK)]
    hb = v1b + v2a[0:1]
    na = jnp.zeros((SUBLANES, LANES), F32)
    nb = jnp.zeros((SUBLANES, LANES), F32)
    zsum = None
    top = None
    for t in range(PEER_TOPK):
        head = lists[0]
        m = jnp.max(jnp.maximum(head, hb), axis=0, keepdims=True)
        ia = jnp.where(head == m, row8, 99.0)
        ib = jnp.where(hb == m, row8 + float(SUBLANES), 99.0)
        first = jnp.min(jnp.minimum(ia, ib), axis=0, keepdims=True)
        popa = row8 == first
        popb = (row8 + float(SUBLANES)) == first
        if t == 0:
            top = m
            zsum = jnp.ones_like(m)
        else:
            zsum = zsum + jnp.exp(m - top)
        na = na + jnp.where(popa, 1.0, 0.0)
        nb = nb + jnp.where(popb, 1.0, 0.0)
        lists = [jnp.where(popa, lists[k + 1], lists[k]) for k in range(PEER_TOPK - 1 - t)]
        hb = jnp.where(popb, NEG_INF, hb)
    s1p = s_ref[0, hp, ltp]
    s2p = s_ref[1, hp, ltp]
    n1 = jnp.zeros((PEER_NKEYS, LANES), F32)
    for r in range(PEER_TOPK):
        n1 = jnp.where(s1p == rows16(v1a, v1b, r), rows16(na, nb, r), n1)
    n1_ref[thp, chp, :, lop] = n1
    e2_ref[thp, chp, :, lop] = (jnp.exp(s2p - v2a[0:1]) * (1.0 / zsum)).astype(BF16)
    return nxt


def _route_kernel(ym_ref, x0_ref, mod_ref, wout_ref, g1_ref, b1_ref, wq_ref, k1_ref, k2_ref,
                  x1_ref, h2t_ref, s_ref, *, ts):
    nlt = ts // LANES
    mod = mod_ref[0]
    y = jnp.dot(ym_ref[0], wout_ref[...], preferred_element_type=F32)
    x1 = _layer_norm(ALPHA * x0_ref[0] + mod[2:3] * y, g1_ref[...], b1_ref[...])
    x1_ref[0] = x1
    h2 = x1 * (1.0 + mod[4:5]) + mod[3:4]
    h2t_ref[...] = h2.T.astype(BF16)
    q = jnp.dot(h2.astype(BF16), wq_ref[...], preferred_element_type=F32)
    half = PEER_DKEY // 2
    nt = (((1,), (1,)), ((), ()))
    for hh in range(PEER_HEADS):
        qh = q[:, hh * PEER_DKEY:(hh + 1) * PEER_DKEY]
        mu = jnp.mean(qh, axis=-1, keepdims=True)
        qc = qh - mu
        var = jnp.mean(qc * qc, axis=-1, keepdims=True)
        qb = (qc * lax.rsqrt(var + LN_EPS)).astype(BF16)
        s1t = lax.dot_general(k1_ref[...], qb[:, :half], nt, preferred_element_type=F32)
        s2t = lax.dot_general(k2_ref[...], qb[:, half:], nt, preferred_element_type=F32)
        for lt in range(nlt):
            s_ref[0, hh, lt] = s1t[:, lt * LANES:(lt + 1) * LANES]
            s_ref[1, hh, lt] = s2t[:, lt * LANES:(lt + 1) * LANES]


def _route_call(ymix, x0, mod3, w_out, ln1_g, ln1_b, w_q, k1, k2, ts):
    B, S, D = x0.shape
    T = B * S
    nS = S // ts
    nlt = ts // LANES
    dq = w_q.shape[1]
    full = lambda shape: pl.BlockSpec(shape, lambda b, s: (0,) * len(shape))
    return pl.pallas_call(
        functools.partial(_route_kernel, ts=ts),
        grid=(B, nS),
        in_specs=[pl.BlockSpec((1, ts, D), lambda b, s: (b, s, 0)),
                  pl.BlockSpec((1, ts, D), lambda b, s: (b, s, 0)),
                  pl.BlockSpec((1, 6, D), lambda b, s: (b, 0, 0)),
                  full((D, D)), full((1, D)), full((1, D)), full((D, dq)),
                  full((PEER_NKEYS, PEER_DKEY // 2)), full((PEER_NKEYS, PEER_DKEY // 2))],
        out_specs=[pl.BlockSpec((1, ts, D), lambda b, s: (b, s, 0)),
                   pl.BlockSpec((D, ts), lambda b, s: (0, b * nS + s)),
                   pl.BlockSpec((2, PEER_HEADS, nlt, PEER_NKEYS, LANES), lambda b, s: (0, 0, b * nS + s, 0, 0))],
        out_shape=[jax.ShapeDtypeStruct((B, S, D), F32),
                   jax.ShapeDtypeStruct((D, T), BF16),
                   jax.ShapeDtypeStruct((2, PEER_HEADS, T // LANES, PEER_NKEYS, LANES), F32)],
        compiler_params=pltpu.CompilerParams(
            dimension_semantics=("arbitrary", "arbitrary"), vmem_limit_bytes=VMEM_LIMIT),
        name="route",
    )(ymix, x0, mod3, w_out, ln1_g, ln1_b, w_q, k1, k2)


def _peer_kernel(h2t_ref, u_ref, vt_ref, s_ref, x1_ref, mod_ref, g2_ref, b2_ref, o_ref,
                 acc_ref, p_ref, act_ref, krow_ref, r2a, e2a, n1a, e1a, r2b, e2b, n1b, e1b, *, tm, te):
    i = pl.program_id(0)
    j = pl.program_id(1)
    nct = tm // TOK_CHUNK
    nlt = tm // LANES
    nblk = te // PEER_NKEYS
    grp_rows = BLK_GROUP * PEER_NKEYS
    zero = jnp.zeros((BF16_ROWS, TOK_CHUNK), BF16)

    def route(tabs):
        carry = None
        for q in range(nlt + 1):
            carry = _route_unit(s_ref, tabs, nlt, q if q < nlt else None, q - 1 if q > 0 else None,
                                carry, tab_head=j)

    def dense(tabs):
        rank2_t, e2_t, n1_t, e1_t = tabs
        k0 = pl.multiple_of(j * nblk, nblk)
        krow_ref[0] = n1_t[:, :, pl.ds(k0, nblk), :]
        krow_ref[1] = e1_t[:, :, pl.ds(k0, nblk), :]

        def act_group(b0):
            rs = slice(b0 * PEER_NKEYS, b0 * PEER_NKEYS + grp_rows)
            act_ref[rs, :] = jnp.dot(u_ref[rs, :], h2t_ref[...], preferred_element_type=F32)

        def key_row(t, hh, c, b):
            halves = [jnp.broadcast_to(krow_ref[t, hh, c, b:b + 1, l0:l0 + LANES],
                                       (BF16_ROWS, LANES)).astype(BF16)
                      for l0 in range(0, TOK_CHUNK, LANES)]
            return jnp.concatenate(halves, axis=1)

        act_group(0)
        for b0 in range(0, nblk, BLK_GROUP):
            blks = range(b0, b0 + BLK_GROUP)
            if b0 + BLK_GROUP < nblk:
                act_group(b0 + BLK_GROUP)
            for c in range(nct):
                cs = slice(c * TOK_CHUNK, (c + 1) * TOK_CHUNK)
                w = {(b, k): zero for b in blks for k in range(NPIECE)}
                for hh in range(PEER_HEADS):
                    n1v = {b: key_row(0, hh, c, b) for b in blks}
                    e1v = {b: key_row(1, hh, c, b) for b in blks}
                    for k in range(NPIECE):
                        ks = slice(k * BF16_ROWS, (k + 1) * BF16_ROWS)
                        r2 = rank2_t[hh, c, ks, :]
                        e2 = e2_t[hh, c, ks, :]
                        for b in blks:
                            w[b, k] = w[b, k] + jnp.where(r2 < n1v[b], e2, zero) * e1v[b]
                for b in blks:
                    rs = slice(b * PEER_NKEYS, (b + 1) * PEER_NKEYS)
                    wb = jnp.concatenate([w[b, k] for k in range(NPIECE)], axis=0)
                    p_ref[rs, cs] = wb * _gelu(act_ref[rs, cs]).astype(BF16)
        acc_ref[...] += jnp.dot(vt_ref[...], p_ref[...], preferred_element_type=F32)

    def step(tabs_cur, tabs_next):
        @pl.when(j == 0)
        def _():
            acc_ref[...] = jnp.zeros_like(acc_ref)

        route(tabs_next)
        dense(tabs_cur)

        @pl.when(j == pl.num_programs(1) - 1)
        def _():
            mod = mod_ref[0]
            y = acc_ref[...].T
            o_ref[...] = _layer_norm(ALPHA * x1_ref[...] + mod[5:6] * y, g2_ref[...], b2_ref[...])

    tabs_a = (r2a, e2a, n1a, e1a)
    tabs_b = (r2b, e2b, n1b, e1b)

    @pl.when(i == 0)
    def _():
        route(tabs_a)

    @pl.when(jnp.logical_and(i > 0, i % 2 == 1))
    def _():
        step(tabs_a, tabs_b)

    @pl.when(jnp.logical_and(i > 0, i % 2 == 0))
    def _():
        step(tabs_b, tabs_a)


def _peer_call(h2t, u_b, vt_b, scores, x1f, mod3, ln2_g, ln2_b, S, tm, te):
    D, T = h2t.shape
    ne = u_b.shape[0]
    assert ne // te == PEER_HEADS, "one routing head per expert-tile step"
    nct = tm // TOK_CHUNK
    nlt = tm // LANES
    nblk = te // PEER_NKEYS
    ntile = T // tm
    tiles_per_batch = S // tm
    full = lambda shape: pl.BlockSpec(shape, lambda i, j: (0,) * len(shape))
    prev = lambda i: jnp.maximum(i - 1, 0)
    tab = lambda dt: pltpu.VMEM((PEER_HEADS, nct, PEER_NKEYS, TOK_CHUNK), dt)
    return pl.pallas_call(
        functools.partial(_peer_kernel, tm=tm, te=te),
        grid=(ntile + 1, ne // te),
        in_specs=[pl.BlockSpec((D, tm), lambda i, j: (0, prev(i))),
                  pl.BlockSpec((te, D), lambda i, j: (j, 0)),
                  pl.BlockSpec((D, te), lambda i, j: (0, j)),
                  pl.BlockSpec((2, 1, nlt, PEER_NKEYS, LANES),
                               lambda i, j: (0, j, jnp.minimum(i, ntile - 1), 0, 0)),
                  pl.BlockSpec((tm, D), lambda i, j: (prev(i), 0)),
                  pl.BlockSpec((1, 6, D), lambda i, j: (prev(i) // tiles_per_batch, 0, 0)),
                  full((1, D)), full((1, D))],
        out_specs=pl.BlockSpec((tm, D), lambda i, j: (prev(i), 0)),
        out_shape=jax.ShapeDtypeStruct((T, D), F32),
        scratch_shapes=[pltpu.VMEM((D, tm), F32),
                        pltpu.VMEM((te, tm), BF16),
                        pltpu.VMEM((te, tm), F32),
                        pltpu.VMEM((2, PEER_HEADS, nct, nblk, TOK_CHUNK), F32),
                        tab(BF16), tab(BF16), tab(F32), tab(F32),
                        tab(BF16), tab(BF16), tab(F32), tab(F32)],
        compiler_params=pltpu.CompilerParams(
            dimension_semantics=("arbitrary", "arbitrary"), vmem_limit_bytes=VMEM_LIMIT),
        name="peer",
    )(h2t, u_b, vt_b, scores, x1f, mod3, ln2_g, ln2_b)


def _block_diag(w):
    nh, hd, _ = w.shape
    eye = jnp.eye(nh, dtype=w.dtype)
    return (eye[:, None, :, None] * w[:, :, None, :]).reshape(nh * hd, nh * hd)


def kernel(x, c, ln0_g, ln0_b, ada_w, ada_b, w_in, pool_w, pool_scale, conv_w, conv_b, lru_wa, lru_ba, lru_wx, lru_bx, lru_lambda, w_out, ln1_g, ln1_b, peer_wq, peer_k1, peer_k2, peer_u, peer_v, ln2_g, ln2_b):
    B, S, D = x.shape
    ts, tm, te = SEQ_TILE, TOK_TILE, EXPERT_TILE
    T = B * S
    row = lambda v: v.reshape(1, -1)
    l = 0
    c_pad = jnp.pad(c, ((0, SUBLANES - B), (0, 0)))
    mod = _mod_call(c_pad, ada_w[l], row(ada_b[l]))[:B]
    mod3 = mod.reshape(B, 6, D)

    wg = jnp.concatenate([_block_diag(lru_wa[l]), _block_diag(lru_wx[l])], axis=1).astype(BF16)
    bg = jnp.concatenate([lru_ba[l].reshape(1, -1), lru_bx[l].reshape(1, -1)], axis=1)
    x0, ymix = _mixer_call(x, mod3, row(ln0_g), row(ln0_b), w_in[l].astype(BF16),
                           pool_w[l].astype(BF16), row(pool_scale[l]),
                           conv_w[l].reshape(CONV_W, D_LRU), row(conv_b[l]), wg, bg,
                           row(lru_lambda[l]), ts)
    x1, h2t, scores = _route_call(
        ymix, x0, mod3, w_out[l].astype(BF16), row(ln1_g[l]), row(ln1_b[l]),
        peer_wq[l].astype(BF16), peer_k1[l].astype(BF16), peer_k2[l].astype(BF16), ts)
    out = _peer_call(h2t, peer_u[l].astype(BF16), peer_v[l].T.astype(BF16), scores,
                     x1.reshape(T, D), mod3, row(ln2_g[l]), row(ln2_b[l]), S, tm, te)
    return out.reshape(B, S, D)
```

```python
import functools

import jax
import jax.numpy as jnp
from jax import lax
from jax.experimental import pallas as pl
from jax.experimental.pallas import tpu as pltpu

F32 = jnp.float32
BF16 = jnp.bfloat16

D_MODEL = 1024
D_POOL = 512
D_LRU = 512
POOL_WINDOWS = (2, 4, 8, 16)
POOL_GW = 128
POOL_HALO = 16
LRU_HEADS = 8
LRU_HD = 64
CONV_W = 4
CONV_HALO = 8
LRU_C = 8.0
PEER_HEADS = 8
PEER_NKEYS = 128
PEER_DKEY = 256
PEER_TOPK = 16
PEER_N = PEER_NKEYS * PEER_NKEYS
LN_EPS = 1e-5
DEPTH = 1
ALPHA = (2.0 * DEPTH) ** 0.25

LANES = 128
SUBLANES = 8
BF16_ROWS = 2 * SUBLANES
NPIECE = PEER_NKEYS // BF16_ROWS
TOK_CHUNK = 2 * LANES
BLK_GROUP = 2
VMEM_LIMIT = 56 * 1024 * 1024
SEQ_TILE = 512
TOK_TILE = 512
EXPERT_TILE = PEER_N // PEER_HEADS
NEG_INF = float("-inf")


def _layer_norm(x, g, b):
    mu = jnp.mean(x, axis=-1, keepdims=True)
    xc = x - mu
    var = jnp.mean(xc * xc, axis=-1, keepdims=True)
    return xc * lax.rsqrt(var + LN_EPS) * g + b


_GELU_A = -2.0 * 0.7978845608028654 * 1.4426950408889634
_GELU_B = _GELU_A * 0.044715


def _gelu(x):
    return x / (1.0 + jnp.exp2(x * (_GELU_A + _GELU_B * (x * x))))


def _sigmoid(x):
    return 1.0 / (1.0 + jnp.exp(-x))


def _mod_kernel(c_ref, w_ref, b_ref, o_ref):
    c = c_ref[...]
    cond = c * _sigmoid(c)
    o_ref[...] = jnp.dot(cond, w_ref[...], preferred_element_type=F32) + b_ref[...]


def _mod_call(c_pad, ada_w, ada_b, tn=1024):
    rows, d = c_pad.shape
    n = ada_w.shape[1]
    return pl.pallas_call(
        _mod_kernel,
        grid=(n // tn,),
        in_specs=[pl.BlockSpec((rows, d), lambda j: (0, 0)),
                  pl.BlockSpec((d, tn), lambda j: (0, j)),
                  pl.BlockSpec((1, tn), lambda j: (0, j))],
        out_specs=pl.BlockSpec((rows, tn), lambda j: (0, j)),
        out_shape=jax.ShapeDtypeStruct((rows, n), F32),
        name="mod",
    )(c_pad, ada_w, ada_b)


def _front_kernel(x_ref, modm_ref, modr_ref, g0_ref, b0_ref, win_ref, pw_ref, ps_ref, cw_ref, cb_ref,
                  wg_ref, bg_ref, lam_ref, wout_ref, g1_ref, b1_ref, wq_ref, k1_ref, k2_ref,
                  x1_ref, h2t_ref, s_ref,
                  zp_ext, zx_ext, a_s, b_s, h_s, hc_ref, x0_prev, ym_prev, *, ts, nseq):
    k = pl.program_id(0)
    s = jnp.minimum(k, pl.num_programs(0) - 2) % nseq

    @pl.when(k == 0)
    def _():
        x0_prev[...] = jnp.zeros_like(x0_prev)
        ym_prev[...] = jnp.zeros_like(ym_prev)

    @pl.when(s == 0)
    def _():
        zp_ext[0:POOL_HALO, :] = jnp.zeros((POOL_HALO, D_POOL), F32)
        zx_ext[0:CONV_HALO, :] = jnp.zeros((CONV_HALO, D_LRU), F32)
        hc_ref[...] = jnp.zeros_like(hc_ref)

    @pl.when(s > 0)
    def _():
        zp_ext[0:POOL_HALO, :] = zp_ext[ts:ts + POOL_HALO, :]
        zx_ext[0:CONV_HALO, :] = zx_ext[ts:ts + CONV_HALO, :]

    nlt = ts // LANES
    modr = modr_ref[0]
    y = jnp.dot(ym_prev[...], wout_ref[...], preferred_element_type=F32)
    x1 = _layer_norm(ALPHA * x0_prev[...] + modr[2:3] * y, g1_ref[...], b1_ref[...])
    x1_ref[0] = x1
    h2 = x1 * (1.0 + modr[4:5]) + modr[3:4]
    h2t_ref[...] = h2.T.astype(BF16)
    q = jnp.dot(h2.astype(BF16), wq_ref[...], preferred_element_type=F32)
    half = PEER_DKEY // 2
    nt = (((1,), (1,)), ((), ()))
    for hh in range(PEER_HEADS):
        qh = q[:, hh * PEER_DKEY:(hh + 1) * PEER_DKEY]
        mu = jnp.mean(qh, axis=-1, keepdims=True)
        qc = qh - mu
        var = jnp.mean(qc * qc, axis=-1, keepdims=True)
        qb = (qc * lax.rsqrt(var + LN_EPS)).astype(BF16)
        s1t = lax.dot_general(k1_ref[...], qb[:, :half], nt, preferred_element_type=F32)
        s2t = lax.dot_general(k2_ref[...], qb[:, half:], nt, preferred_element_type=F32)
        for lt in range(nlt):
            s_ref[0, hh, lt] = s1t[:, lt * LANES:(lt + 1) * LANES]
            s_ref[1, hh, lt] = s2t[:, lt * LANES:(lt + 1) * LANES]

    x0 = _layer_norm(x_ref[0], g0_ref[...], b0_ref[...])
    mod = modm_ref[0]
    h = x0 * (1.0 + mod[1:2]) + mod[0:1]
    z = jnp.dot(h.astype(BF16), win_ref[...], preferred_element_type=F32)
    zp_ext[POOL_HALO:, :] = z[:, :D_POOL]
    zx_ext[CONV_HALO:, :] = z[:, D_POOL:D_POOL + D_LRU]
    zg = z[:, D_POOL + D_LRU:]

    t_glob = lax.broadcasted_iota(jnp.int32, (ts, POOL_GW), 0) + s * ts
    outs = []
    for g, w in enumerate(POOL_WINDOWS):
        cs = slice(g * POOL_GW, (g + 1) * POOL_GW)
        cur = zp_ext[POOL_HALO:POOL_HALO + ts, cs]
        acc = cur
        for k in range(1, w):
            acc = acc + zp_ext[POOL_HALO - k:POOL_HALO - k + ts, cs]
        cnt = jnp.minimum(t_glob + 1, w).astype(F32)
        y = acc / cnt - cur
        outs.append(jnp.dot(y.astype(BF16), pw_ref[g], preferred_element_type=F32))
    y_pool = jnp.concatenate(outs, axis=1) * ps_ref[...]

    cw = cw_ref[...]
    xc = cb_ref[...] + cw[0:1] * zx_ext[CONV_HALO - 3:CONV_HALO - 3 + ts, :]
    for k in range(1, CONV_W):
        xc = xc + cw[k:k + 1] * zx_ext[CONV_HALO - 3 + k:CONV_HALO - 3 + k + ts, :]
    gates = jnp.dot(xc.astype(BF16), wg_ref[...], preferred_element_type=F32) + bg_ref[...]
    r = _sigmoid(gates[:, :D_LRU])
    i = _sigmoid(gates[:, D_LRU:])
    nl = -lam_ref[...]
    softplus = jnp.maximum(nl, 0.0) + jnp.log(1.0 + jnp.exp(-jnp.abs(nl)))
    a = jnp.exp((-LRU_C * softplus) * r)
    om = 1.0 - a * a
    bv = jnp.where(om > 0.0, om * lax.rsqrt(om), 0.0) * (i * xc)

    a = a.reshape(ts // SUBLANES, SUBLANES, D_LRU)
    bv = bv.reshape(ts // SUBLANES, SUBLANES, D_LRU)
    row_in_grp = lax.broadcasted_iota(jnp.int32, a.shape, 1)
    for k in (1, 2, 4):
        a_sh = pltpu.roll(a, k, 1)
        b_sh = pltpu.roll(bv, k, 1)
        valid = row_in_grp >= k
        bv = jnp.where(valid, a * b_sh + bv, bv)
        a = jnp.where(valid, a * a_sh, a)
    a_s[...] = a.reshape(ts, D_LRU)
    b_s[...] = bv.reshape(ts, D_LRU)

    def grp(gi, hprev):
        off = pl.multiple_of(gi * SUBLANES, SUBLANES)
        hg = a_s[pl.ds(off, SUBLANES), :] * hprev + b_s[pl.ds(off, SUBLANES), :]
        h_s[pl.ds(off, SUBLANES), :] = hg
        return hg[SUBLANES - 1:SUBLANES, :]

    hc_ref[0:1, :] = lax.fori_loop(0, ts // SUBLANES, grp, hc_ref[0:1, :])
    y_lru = h_s[...] * _gelu(zg)
    x0_prev[...] = x0
    ym_prev[...] = jnp.concatenate([y_pool, y_lru], axis=1).astype(BF16)


def _front_call(x, mod3, ln0_g, ln0_b, w_in, pool_w, pool_scale, conv_w, conv_b, wg, bg, lam,
                w_out, ln1_g, ln1_b, w_q, k1, k2, ts):
    B, S, D = x.shape
    T = B * S
    nseq = S // ts
    ntile = B * nseq
    nlt = ts // LANES
    dz = w_in.shape[1]
    dq = w_q.shape[1]
    full = lambda shape: pl.BlockSpec(shape, lambda k: (0,) * len(shape))
    cur = lambda k: jnp.minimum(k, ntile - 1)
    prev = lambda k: jnp.maximum(k - 1, 0)
    return pl.pallas_call(
        functools.partial(_front_kernel, ts=ts, nseq=nseq),
        grid=(ntile + 1,),
        in_specs=[pl.BlockSpec((1, ts, D), lambda k: (cur(k) // nseq, cur(k) % nseq, 0)),
                  pl.BlockSpec((1, 6, D), lambda k: (cur(k) // nseq, 0, 0)),
                  pl.BlockSpec((1, 6, D), lambda k: (prev(k) // nseq, 0, 0)),
                  full((1, D)), full((1, D)), full((D, dz)),
                  full((len(POOL_WINDOWS), POOL_GW, POOL_GW)), full((1, D_POOL)),
                  full((CONV_W, D_LRU)), full((1, D_LRU)),
                  full((D_LRU, 2 * D_LRU)), full((1, 2 * D_LRU)), full((1, D_LRU)),
                  full((D, D)), full((1, D)), full((1, D)), full((D, dq)),
                  full((PEER_NKEYS, PEER_DKEY // 2)), full((PEER_NKEYS, PEER_DKEY // 2))],
        out_specs=[pl.BlockSpec((1, ts, D), lambda k: (prev(k) // nseq, prev(k) % nseq, 0)),
                   pl.BlockSpec((D, ts), lambda k: (0, prev(k))),
                   pl.BlockSpec((2, PEER_HEADS, nlt, PEER_NKEYS, LANES), lambda k: (0, 0, prev(k), 0, 0))],
        out_shape=[jax.ShapeDtypeStruct((B, S, D), F32),
                   jax.ShapeDtypeStruct((D, T), BF16),
                   jax.ShapeDtypeStruct((2, PEER_HEADS, T // LANES, PEER_NKEYS, LANES), F32)],
        scratch_shapes=[pltpu.VMEM((POOL_HALO + ts, D_POOL), F32),
                        pltpu.VMEM((CONV_HALO + ts, D_LRU), F32),
                        pltpu.VMEM((ts, D_LRU), F32),
                        pltpu.VMEM((ts, D_LRU), F32),
                        pltpu.VMEM((ts, D_LRU), F32),
                        pltpu.VMEM((SUBLANES, D_LRU), F32),
                        pltpu.VMEM((ts, D), F32),
                        pltpu.VMEM((ts, D), BF16)],
        compiler_params=pltpu.CompilerParams(
            dimension_semantics=("arbitrary",), vmem_limit_bytes=VMEM_LIMIT),
        name="front",
    )(x, mod3, mod3, ln0_g, ln0_b, w_in, pool_w, pool_scale, conv_w, conv_b, wg, bg, lam,
      w_out, ln1_g, ln1_b, w_q, k1, k2)


def _oddeven_merge_sort_pairs(n):
    pairs = []
    p = 1
    while p < n:
        k = p
        while k >= 1:
            for j in range(k % p, n - k, 2 * k):
                for i in range(min(k, n - j - k)):
                    if (i + j) // (2 * p) == (i + j + k) // (2 * p):
                        pairs.append((i + j, i + j + k))
            k //= 2
        p *= 2
    return pairs


_SORT16 = _oddeven_merge_sort_pairs(PEER_TOPK)


def _top16_sorted(rows):
    v = list(rows)
    for i, j in _SORT16:
        v[i], v[j] = jnp.maximum(v[i], v[j]), jnp.minimum(v[i], v[j])
    shift = SUBLANES // 2
    while shift >= 1:
        other = [pltpu.roll(x, shift, 0) for x in v]
        v = [jnp.maximum(v[k], other[PEER_TOPK - 1 - k]) for k in range(PEER_TOPK)]
        d = PEER_TOPK // 2
        while d >= 1:
            for i in range(PEER_TOPK):
                if i & d == 0:
                    v[i], v[i + d] = jnp.maximum(v[i], v[i + d]), jnp.minimum(v[i], v[i + d])
            d //= 2
        shift //= 2
    return v


def _route_unit(s_ref, tabs, nlt, u_ext, u_mrg, carry, tab_head=None):
    rank2_ref, e2_ref, n1_ref, e1_ref = tabs
    row8 = lax.broadcasted_iota(jnp.int32, (SUBLANES, LANES), 0).astype(F32)
    per_chunk = TOK_CHUNK // LANES

    def unit_slot(u):
        hh = u // nlt
        lt = u % nlt
        if isinstance(u, int):
            lo = slice((lt % per_chunk) * LANES, (lt % per_chunk + 1) * LANES)
        else:
            lo = pl.ds(pl.multiple_of((lt % per_chunk) * LANES, LANES), LANES)
        return hh, lt, (hh if tab_head is None else tab_head), lt // per_chunk, lo

    def rows16(va, vb, r):
        return va[r:r + 1] if r < SUBLANES else vb[r - SUBLANES:r - SUBLANES + 1]

    nxt = None
    if u_ext is not None:
        hh, lt, th, ch, lo = unit_slot(u_ext)
        s1 = s_ref[0, hh, lt]
        s2 = s_ref[1, hh, lt]
        nrow = PEER_NKEYS // SUBLANES
        rows2 = [s2[k * SUBLANES:(k + 1) * SUBLANES] for k in range(nrow)]
        vals1 = _top16_sorted([s1[k * SUBLANES:(k + 1) * SUBLANES] for k in range(nrow)])
        vals2 = _top16_sorted(rows2)
        rank2 = []
        for row in rows2:
            cnt = jnp.where(vals2[0] > row, 1.0, 0.0)
            for r in range(1, PEER_TOPK):
                cnt = cnt + jnp.where(vals2[r] > row, 1.0, 0.0)
            rank2.append(cnt)
        rank2_ref[th, ch, :, lo] = jnp.concatenate(rank2, axis=0).astype(BF16)
        e1_ref[th, ch, :, lo] = jnp.exp(s1 - vals1[0][0:1])
        nxt = [jnp.zeros((SUBLANES, LANES), F32) for _ in range(4)]
        for r in range(SUBLANES):
            pick = row8 == float(r)
            nxt[0] = jnp.where(pick, vals1[r], nxt[0])
            nxt[1] = jnp.where(pick, vals1[r + SUBLANES], nxt[1])
            nxt[2] = jnp.where(pick, vals2[r], nxt[2])
            nxt[3] = jnp.where(pick, vals2[r + SUBLANES], nxt[3])
    if u_mrg is None:
        return nxt
    v1a, v1b, v2a, v2b = carry

    hp, ltp, thp, chp, lop = unit_slot(u_mrg)
    lists = [v1a + rows16(v2a, v2b, r) for r in range(PEER_TOPK)]
    hb = v1b + v2a[0:1]
    na = jnp.zeros((SUBLANES, LANES), F32)
    nb = jnp.zeros((SUBLANES, LANES), F32)
    zsum = None
    top = None
    for t in range(PEER_TOPK):
        head = lists[0]
        m = jnp.max(jnp.maximum(head, hb), axis=0, keepdims=True)
        ia = jnp.where(head == m, row8, 99.0)
        ib = jnp.where(hb == m, row8 + float(SUBLANES), 99.0)
        first = jnp.min(jnp.minimum(ia, ib), axis=0, keepdims=True)
        popa = row8 == first
        popb = (row8 + float(SUBLANES)) == first
        if t == 0:
            top = m
            zsum = jnp.ones_like(m)
        else:
            zsum = zsum + jnp.exp(m - top)
        na = na + jnp.where(popa, 1.0, 0.0)
        nb = nb + jnp.where(popb, 1.0, 0.0)
        lists = [jnp.where(popa, lists[k + 1], lists[k]) for k in range(PEER_TOPK - 1 - t)]
        hb = jnp.where(popb, NEG_INF, hb)
    s1p = s_ref[0, hp, ltp]
    s2p = s_ref[1, hp, ltp]
    n1 = jnp.zeros((PEER_NKEYS, LANES), F32)
    for r in range(PEER_TOPK):
        n1 = jnp.where(s1p == rows16(v1a, v1b, r), rows16(na, nb, r), n1)
    n1_ref[thp, chp, :, lop] = n1
    e2_ref[thp, chp, :, lop] = (jnp.exp(s2p - v2a[0:1]) * (1.0 / zsum)).astype(BF16)
    return nxt


def _peer_kernel(h2t_ref, u_ref, vt_ref, s_ref, x1_ref, mod_ref, g2_ref, b2_ref, o_ref,
                 acc_ref, p_ref, act_ref, krow_ref, r2a, e2a, n1a, e1a, r2b, e2b, n1b, e1b, *, tm, te):
    i = pl.program_id(0)
    j = pl.program_id(1)
    nct = tm // TOK_CHUNK
    nlt = tm // LANES
    nblk = te // PEER_NKEYS
    grp_rows = BLK_GROUP * PEER_NKEYS
    zero = jnp.zeros((BF16_ROWS, TOK_CHUNK), BF16)

    def route(tabs):
        carry = None
        for q in range(nlt + 1):
            carry = _route_unit(s_ref, tabs, nlt, q if q < nlt else None, q - 1 if q > 0 else None,
                                carry, tab_head=j)

    def dense(tabs):
        rank2_t, e2_t, n1_t, e1_t = tabs
        k0 = pl.multiple_of(j * nblk, nblk)
        krow_ref[0] = n1_t[:, :, pl.ds(k0, nblk), :]
        krow_ref[1] = e1_t[:, :, pl.ds(k0, nblk), :]

        def act_group(b0):
            rs = slice(b0 * PEER_NKEYS, b0 * PEER_NKEYS + grp_rows)
            act_ref[rs, :] = jnp.dot(u_ref[rs, :], h2t_ref[...], preferred_element_type=F32)

        def key_row(t, hh, c, b):
            halves = [jnp.broadcast_to(krow_ref[t, hh, c, b:b + 1, l0:l0 + LANES],
                                       (BF16_ROWS, LANES)).astype(BF16)
                      for l0 in range(0, TOK_CHUNK, LANES)]
            return jnp.concatenate(halves, axis=1)

        act_group(0)
        for b0 in range(0, nblk, BLK_GROUP):
            blks = range(b0, b0 + BLK_GROUP)
            if b0 + BLK_GROUP < nblk:
                act_group(b0 + BLK_GROUP)
            for c in range(nct):
                cs = slice(c * TOK_CHUNK, (c + 1) * TOK_CHUNK)
                w = {(b, k): zero for b in blks for k in range(NPIECE)}
                for hh in range(PEER_HEADS):
                    n1v = {b: key_row(0, hh, c, b) for b in blks}
                    e1v = {b: key_row(1, hh, c, b) for b in blks}
                    for k in range(NPIECE):
                        ks = slice(k * BF16_ROWS, (k + 1) * BF16_ROWS)
                        r2 = rank2_t[hh, c, ks, :]
                        e2 = e2_t[hh, c, ks, :]
                        for b in blks:
                            w[b, k] = w[b, k] + jnp.where(r2 < n1v[b], e2, zero) * e1v[b]
                for b in blks:
                    rs = slice(b * PEER_NKEYS, (b + 1) * PEER_NKEYS)
                    wb = jnp.concatenate([w[b, k] for k in range(NPIECE)], axis=0)
                    p_ref[rs, cs] = wb * _gelu(act_ref[rs, cs]).astype(BF16)
        acc_ref[...] += jnp.dot(vt_ref[...], p_ref[...], preferred_element_type=F32)

    def step(tabs_cur, tabs_next):
        @pl.when(j == 0)
        def _():
            acc_ref[...] = jnp.zeros_like(acc_ref)

        route(tabs_next)
        dense(tabs_cur)

        @pl.when(j == pl.num_programs(1) - 1)
        def _():
            mod = mod_ref[0]
            y = acc_ref[...].T
            o_ref[...] = _layer_norm(ALPHA * x1_ref[...] + mod[5:6] * y, g2_ref[...], b2_ref[...])

    tabs_a = (r2a, e2a, n1a, e1a)
    tabs_b = (r2b, e2b, n1b, e1b)

    @pl.when(i == 0)
    def _():
        route(tabs_a)

    @pl.when(jnp.logical_and(i > 0, i % 2 == 1))
    def _():
        step(tabs_a, tabs_b)

    @pl.when(jnp.logical_and(i > 0, i % 2 == 0))
    def _():
        step(tabs_b, tabs_a)


def _peer_call(h2t, u_b, vt_b, scores, x1f, mod3, ln2_g, ln2_b, S, tm, te):
    D, T = h2t.shape
    ne = u_b.shape[0]
    assert ne // te == PEER_HEADS, "one routing head per expert-tile step"
    nct = tm // TOK_CHUNK
    nlt = tm // LANES
    nblk = te // PEER_NKEYS
    ntile = T // tm
    tiles_per_batch = S // tm
    full = lambda shape: pl.BlockSpec(shape, lambda i, j: (0,) * len(shape))
    prev = lambda i: jnp.maximum(i - 1, 0)
    tab = lambda dt: pltpu.VMEM((PEER_HEADS, nct, PEER_NKEYS, TOK_CHUNK), dt)
    return pl.pallas_call(
        functools.partial(_peer_kernel, tm=tm, te=te),
        grid=(ntile + 1, ne // te),
        in_specs=[pl.BlockSpec((D, tm), lambda i, j: (0, prev(i))),
                  pl.BlockSpec((te, D), lambda i, j: (j, 0)),
                  pl.BlockSpec((D, te), lambda i, j: (0, j)),
                  pl.BlockSpec((2, 1, nlt, PEER_NKEYS, LANES),
                               lambda i, j: (0, j, jnp.minimum(i, ntile - 1), 0, 0)),
                  pl.BlockSpec((tm, D), lambda i, j: (prev(i), 0)),
                  pl.BlockSpec((1, 6, D), lambda i, j: (prev(i) // tiles_per_batch, 0, 0)),
                  full((1, D)), full((1, D))],
        out_specs=pl.BlockSpec((tm, D), lambda i, j: (prev(i), 0)),
        out_shape=jax.ShapeDtypeStruct((T, D), F32),
        scratch_shapes=[pltpu.VMEM((D, tm), F32),
                        pltpu.VMEM((te, tm), BF16),
                        pltpu.VMEM((te, tm), F32),
                        pltpu.VMEM((2, PEER_HEADS, nct, nblk, TOK_CHUNK), F32),
                        tab(BF16), tab(BF16), tab(F32), tab(F32),
                        tab(BF16), tab(BF16), tab(F32), tab(F32)],
        compiler_params=pltpu.CompilerParams(
            dimension_semantics=("arbitrary", "arbitrary"), vmem_limit_bytes=VMEM_LIMIT),
        name="peer",
    )(h2t, u_b, vt_b, scores, x1f, mod3, ln2_g, ln2_b)


def _block_diag(w):
    nh, hd, _ = w.shape
    eye = jnp.eye(nh, dtype=w.dtype)
    return (eye[:, None, :, None] * w[:, :, None, :]).reshape(nh * hd, nh * hd)


def kernel(x, c, ln0_g, ln0_b, ada_w, ada_b, w_in, pool_w, pool_scale, conv_w, conv_b, lru_wa, lru_ba, lru_wx, lru_bx, lru_lambda, w_out, ln1_g, ln1_b, peer_wq, peer_k1, peer_k2, peer_u, peer_v, ln2_g, ln2_b):
    B, S, D = x.shape
    ts, tm, te = SEQ_TILE, TOK_TILE, EXPERT_TILE
    T = B * S
    row = lambda v: v.reshape(1, -1)
    l = 0
    c_pad = jnp.pad(c, ((0, SUBLANES - B), (0, 0)))
    mod = _mod_call(c_pad, ada_w[l], row(ada_b[l]))[:B]
    mod3 = mod.reshape(B, 6, D)

    wg = jnp.concatenate([_block_diag(lru_wa[l]), _block_diag(lru_wx[l])], axis=1).astype(BF16)
    bg = jnp.concatenate([lru_ba[l].reshape(1, -1), lru_bx[l].reshape(1, -1)], axis=1)
    x1, h2t, scores = _front_call(
        x, mod3, row(ln0_g), row(ln0_b), w_in[l].astype(BF16), pool_w[l].astype(BF16),
        row(pool_scale[l]), conv_w[l].reshape(CONV_W, D_LRU), row(conv_b[l]), wg, bg,
        row(lru_lambda[l]), w_out[l].astype(BF16), row(ln1_g[l]), row(ln1_b[l]),
        peer_wq[l].astype(BF16), peer_k1[l].astype(BF16), peer_k2[l].astype(BF16), ts)
    out = _peer_call(h2t, peer_u[l].astype(BF16), peer_v[l].T.astype(BF16), scores,
                     x1.reshape(T, D), mod3, row(ln2_g[l]), row(ln2_b[l]), S, tm, te)
    return out.reshape(B, S, D)
```

```python
import functools

import jax
import jax.numpy as jnp
from jax import lax
from jax.experimental import pallas as pl
from jax.experimental.pallas import tpu as pltpu

F32 = jnp.float32
BF16 = jnp.bfloat16

D_MODEL = 1024
D_POOL = 512
D_LRU = 512
POOL_WINDOWS = (2, 4, 8, 16)
POOL_GW = 128
POOL_HALO = 16
LRU_HEADS = 8
LRU_HD = 64
CONV_W = 4
CONV_HALO = 8
LRU_C = 8.0
PEER_HEADS = 8
PEER_NKEYS = 128
PEER_DKEY = 256
PEER_TOPK = 16
PEER_N = PEER_NKEYS * PEER_NKEYS
LN_EPS = 1e-5
DEPTH = 1
ALPHA = (2.0 * DEPTH) ** 0.25

LANES = 128
SUBLANES = 8
BF16_ROWS = 2 * SUBLANES
NPIECE = PEER_NKEYS // BF16_ROWS
TOK_CHUNK = 2 * LANES
BLK_GROUP = 2
VMEM_LIMIT = 56 * 1024 * 1024
SEQ_TILE = 512
TOK_TILE = 512
EXPERT_TILE = PEER_N // PEER_HEADS
NEG_INF = float("-inf")


def _layer_norm(x, g, b):
    mu = jnp.mean(x, axis=-1, keepdims=True)
    xc = x - mu
    var = jnp.mean(xc * xc, axis=-1, keepdims=True)
    return xc * lax.rsqrt(var + LN_EPS) * g + b


_GELU_A = -2.0 * 0.7978845608028654 * 1.4426950408889634
_GELU_B = _GELU_A * 0.044715


def _gelu(x):
    return x / (1.0 + jnp.exp2(x * (_GELU_A + _GELU_B * (x * x))))


def _sigmoid(x):
    return 1.0 / (1.0 + jnp.exp(-x))


def _mod_kernel(c_ref, w_ref, b_ref, o_ref):
    c = c_ref[...]
    cond = c * _sigmoid(c)
    o_ref[...] = jnp.dot(cond, w_ref[...], preferred_element_type=F32) + b_ref[...]


def _mod_call(c_pad, ada_w, ada_b, tn=1024):
    rows, d = c_pad.shape
    n = ada_w.shape[1]
    return pl.pallas_call(
        _mod_kernel,
        grid=(n // tn,),
        in_specs=[pl.BlockSpec((rows, d), lambda j: (0, 0)),
                  pl.BlockSpec((d, tn), lambda j: (0, j)),
                  pl.BlockSpec((1, tn), lambda j: (0, j))],
        out_specs=pl.BlockSpec((rows, tn), lambda j: (0, j)),
        out_shape=jax.ShapeDtypeStruct((rows, n), F32),
        name="mod",
    )(c_pad, ada_w, ada_b)


def _front_kernel(x_ref, modm_ref, modr_ref, g0_ref, b0_ref, win_ref, pw_ref, ps_ref, cw_ref, cb_ref,
                  wg_ref, bg_ref, lam_ref, wout_ref, g1_ref, b1_ref, wq_ref, k1_ref, k2_ref,
                  x1_ref, h2t_ref, s_ref,
                  zp_ext, zx_ext, a_s, b_s, h_s, hc_ref, x0_prev, ym_prev, *, ts, nseq):
    k = pl.program_id(0)
    s = jnp.minimum(k, pl.num_programs(0) - 2) % nseq

    @pl.when(k == 0)
    def _():
        x0_prev[...] = jnp.zeros_like(x0_prev)
        ym_prev[...] = jnp.zeros_like(ym_prev)

    @pl.when(s == 0)
    def _():
        zp_ext[0:POOL_HALO, :] = jnp.zeros((POOL_HALO, D_POOL), F32)
        zx_ext[0:CONV_HALO, :] = jnp.zeros((CONV_HALO, D_LRU), F32)
        hc_ref[...] = jnp.zeros_like(hc_ref)

    @pl.when(s > 0)
    def _():
        zp_ext[0:POOL_HALO, :] = zp_ext[ts:ts + POOL_HALO, :]
        zx_ext[0:CONV_HALO, :] = zx_ext[ts:ts + CONV_HALO, :]

    nlt = ts // LANES
    modr = modr_ref[0]
    y = jnp.dot(ym_prev[...], wout_ref[...], preferred_element_type=F32)
    x1 = _layer_norm(ALPHA * x0_prev[...] + modr[2:3] * y, g1_ref[...], b1_ref[...])
    x1_ref[0] = x1
    h2 = x1 * (1.0 + modr[4:5]) + modr[3:4]
    h2t_ref[...] = h2.T.astype(BF16)
    q = jnp.dot(h2.astype(BF16), wq_ref[...], preferred_element_type=F32)
    half = PEER_DKEY // 2
    nt = (((1,), (1,)), ((), ()))
    for hh in range(PEER_HEADS):
        qh = q[:, hh * PEER_DKEY:(hh + 1) * PEER_DKEY]
        mu = jnp.mean(qh, axis=-1, keepdims=True)
        qc = qh - mu
        var = jnp.mean(qc * qc, axis=-1, keepdims=True)
        qb = (qc * lax.rsqrt(var + LN_EPS)).astype(BF16)
        s1t = lax.dot_general(k1_ref[...], qb[:, :half], nt, preferred_element_type=F32)
        s2t = lax.dot_general(k2_ref[...], qb[:, half:], nt, preferred_element_type=F32)
        for lt in range(nlt):
            s_ref[0, hh, lt] = s1t[:, lt * LANES:(lt + 1) * LANES]
            s_ref[1, hh, lt] = s2t[:, lt * LANES:(lt + 1) * LANES]

    x0 = _layer_norm(x_ref[0], g0_ref[...], b0_ref[...])
    mod = modm_ref[0]
    h = x0 * (1.0 + mod[1:2]) + mod[0:1]
    z = jnp.dot(h.astype(BF16), win_ref[...], preferred_element_type=F32)
    zp_ext[POOL_HALO:, :] = z[:, :D_POOL]
    zx_ext[CONV_HALO:, :] = z[:, D_POOL:D_POOL + D_LRU]
    zg = z[:, D_POOL + D_LRU:]

    t_glob = lax.broadcasted_iota(jnp.int32, (ts, POOL_GW), 0) + s * ts
    outs = []
    for g, w in enumerate(POOL_WINDOWS):
        cs = slice(g * POOL_GW, (g + 1) * POOL_GW)
        cur = zp_ext[POOL_HALO:POOL_HALO + ts, cs]
        acc = cur
        for k in range(1, w):
            acc = acc + zp_ext[POOL_HALO - k:POOL_HALO - k + ts, cs]
        cnt = jnp.minimum(t_glob + 1, w).astype(F32)
        y = acc / cnt - cur
        outs.append(jnp.dot(y.astype(BF16), pw_ref[g], preferred_element_type=F32))
    y_pool = jnp.concatenate(outs, axis=1) * ps_ref[...]

    cw = cw_ref[...]
    xc = cb_ref[...] + cw[0:1] * zx_ext[CONV_HALO - 3:CONV_HALO - 3 + ts, :]
    for k in range(1, CONV_W):
        xc = xc + cw[k:k + 1] * zx_ext[CONV_HALO - 3 + k:CONV_HALO - 3 + k + ts, :]
    gates = jnp.dot(xc.astype(BF16), wg_ref[...], preferred_element_type=F32) + bg_ref[...]
    r = _sigmoid(gates[:, :D_LRU])
    i = _sigmoid(gates[:, D_LRU:])
    nl = -lam_ref[...]
    softplus = jnp.maximum(nl, 0.0) + jnp.log(1.0 + jnp.exp(-jnp.abs(nl)))
    a = jnp.exp((-LRU_C * softplus) * r)
    om = 1.0 - a * a
    bv = jnp.where(om > 0.0, om * lax.rsqrt(om), 0.0) * (i * xc)

    a = a.reshape(ts // SUBLANES, SUBLANES, D_LRU)
    bv = bv.reshape(ts // SUBLANES, SUBLANES, D_LRU)
    row_in_grp = lax.broadcasted_iota(jnp.int32, a.shape, 1)
    for k in (1, 2, 4):
        a_sh = pltpu.roll(a, k, 1)
        b_sh = pltpu.roll(bv, k, 1)
        valid = row_in_grp >= k
        bv = jnp.where(valid, a * b_sh + bv, bv)
        a = jnp.where(valid, a * a_sh, a)
    a_s[...] = a.reshape(ts, D_LRU)
    b_s[...] = bv.reshape(ts, D_LRU)

    def grp(gi, hprev):
        off = pl.multiple_of(gi * SUBLANES, SUBLANES)
        hg = a_s[pl.ds(off, SUBLANES), :] * hprev + b_s[pl.ds(off, SUBLANES), :]
        h_s[pl.ds(off, SUBLANES), :] = hg
        return hg[SUBLANES - 1:SUBLANES, :]

    hc_ref[0:1, :] = lax.fori_loop(0, ts // SUBLANES, grp, hc_ref[0:1, :])
    y_lru = h_s[...] * _gelu(zg)
    x0_prev[...] = x0
    ym_prev[...] = jnp.concatenate([y_pool, y_lru], axis=1).astype(BF16)


def _front_call(x, mod3, ln0_g, ln0_b, w_in, pool_w, pool_scale, conv_w, conv_b, wg, bg, lam,
                w_out, ln1_g, ln1_b, w_q, k1, k2, ts):
    B, S, D = x.shape
    T = B * S
    nseq = S // ts
    ntile = B * nseq
    nlt = ts // LANES
    dz = w_in.shape[1]
    dq = w_q.shape[1]
    full = lambda shape: pl.BlockSpec(shape, lambda k: (0,) * len(shape))
    cur = lambda k: jnp.minimum(k, ntile - 1)
    prev = lambda k: jnp.maximum(k - 1, 0)
    return pl.pallas_call(
        functools.partial(_front_kernel, ts=ts, nseq=nseq),
        grid=(ntile + 1,),
        in_specs=[pl.BlockSpec((1, ts, D), lambda k: (cur(k) // nseq, cur(k) % nseq, 0)),
                  pl.BlockSpec((1, 6, D), lambda k: (cur(k) // nseq, 0, 0)),
                  pl.BlockSpec((1, 6, D), lambda k: (prev(k) // nseq, 0, 0)),
                  full((1, D)), full((1, D)), full((D, dz)),
                  full((len(POOL_WINDOWS), POOL_GW, POOL_GW)), full((1, D_POOL)),
                  full((CONV_W, D_LRU)), full((1, D_LRU)),
                  full((D_LRU, 2 * D_LRU)), full((1, 2 * D_LRU)), full((1, D_LRU)),
                  full((D, D)), full((1, D)), full((1, D)), full((D, dq)),
                  full((PEER_NKEYS, PEER_DKEY // 2)), full((PEER_NKEYS, PEER_DKEY // 2))],
        out_specs=[pl.BlockSpec((1, ts, D), lambda k: (prev(k) // nseq, prev(k) % nseq, 0)),
                   pl.BlockSpec((D, ts), lambda k: (0, prev(k))),
                   pl.BlockSpec((2, PEER_HEADS, nlt, PEER_NKEYS, LANES), lambda k: (0, 0, prev(k), 0, 0))],
        out_shape=[jax.ShapeDtypeStruct((B, S, D), F32),
                   jax.ShapeDtypeStruct((D, T), BF16),
                   jax.ShapeDtypeStruct((2, PEER_HEADS, T // LANES, PEER_NKEYS, LANES), F32)],
        scratch_shapes=[pltpu.VMEM((POOL_HALO + ts, D_POOL), F32),
                        pltpu.VMEM((CONV_HALO + ts, D_LRU), F32),
                        pltpu.VMEM((ts, D_LRU), F32),
                        pltpu.VMEM((ts, D_LRU), F32),
                        pltpu.VMEM((ts, D_LRU), F32),
                        pltpu.VMEM((SUBLANES, D_LRU), F32),
                        pltpu.VMEM((ts, D), F32),
                        pltpu.VMEM((ts, D), BF16)],
        compiler_params=pltpu.CompilerParams(
            dimension_semantics=("arbitrary",), vmem_limit_bytes=VMEM_LIMIT),
        name="front",
    )(x, mod3, mod3, ln0_g, ln0_b, w_in, pool_w, pool_scale, conv_w, conv_b, wg, bg, lam,
      w_out, ln1_g, ln1_b, w_q, k1, k2)


def _oddeven_merge_sort_pairs(n):
    pairs = []
    p = 1
    while p < n:
        k = p
        while k >= 1:
            for j in range(k % p, n - k, 2 * k):
                for i in range(min(k, n - j - k)):
                    if (i + j) // (2 * p) == (i + j + k) // (2 * p):
                        pairs.append((i + j, i + j + k))
            k //= 2
        p *= 2
    return pairs


_SORT16 = _oddeven_merge_sort_pairs(PEER_TOPK)


def _top16_sorted(rows):
    v = list(rows)
    for i, j in _SORT16:
        v[i], v[j] = jnp.maximum(v[i], v[j]), jnp.minimum(v[i], v[j])
    shift = SUBLANES // 2
    while shift >= 1:
        other = [pltpu.roll(x, shift, 0) for x in v]
        v = [jnp.maximum(v[k], other[PEER_TOPK - 1 - k]) for k in range(PEER_TOPK)]
        d = PEER_TOPK // 2
        while d >= 1:
            for i in range(PEER_TOPK):
                if i & d == 0:
                    v[i], v[i + d] = jnp.maximum(v[i], v[i + d]), jnp.minimum(v[i], v[i + d])
            d //= 2
        shift //= 2
    return v


def _route_unit(s_ref, tabs, nlt, u_ext, u_mrg, carry, tab_head=None):
    rank2_ref, e2_ref, n1_ref, e1_ref = tabs
    row8 = lax.broadcasted_iota(jnp.int32, (SUBLANES, LANES), 0).astype(F32)
    per_chunk = TOK_CHUNK // LANES

    def unit_slot(u):
        hh = u // nlt
        lt = u % nlt
        if isinstance(u, int):
            lo = slice((lt % per_chunk) * LANES, (lt % per_chunk + 1) * LANES)
        else:
            lo = pl.ds(pl.multiple_of((lt % per_chunk) * LANES, LANES), LANES)
        return hh, lt, (hh if tab_head is None else tab_head), lt // per_chunk, lo

    def rows16(va, vb, r):
        return va[r:r + 1] if r < SUBLANES else vb[r - SUBLANES:r - SUBLANES + 1]

    nxt = None
    if u_ext is not None:
        hh, lt, th, ch, lo = unit_slot(u_ext)
        s1 = s_ref[0, hh, lt]
        s2 = s_ref[1, hh, lt]
        nrow = PEER_NKEYS // SUBLANES
        rows2 = [s2[k * SUBLANES:(k + 1) * SUBLANES] for k in range(nrow)]
        vals1 = _top16_sorted([s1[k * SUBLANES:(k + 1) * SUBLANES] for k in range(nrow)])
        vals2 = _top16_sorted(rows2)
        rank2 = []
        for row in rows2:
            cnt = jnp.where(vals2[0] > row, 1.0, 0.0)
            for r in range(1, PEER_TOPK):
                cnt = cnt + jnp.where(vals2[r] > row, 1.0, 0.0)
            rank2.append(cnt)
        rank2_ref[th, ch, :, lo] = jnp.concatenate(rank2, axis=0).astype(BF16)
        e1_ref[th, ch, :, lo] = jnp.exp(s1 - vals1[0][0:1])
        nxt = [jnp.zeros((SUBLANES, LANES), F32) for _ in range(4)]
        for r in range(SUBLANES):
            pick = row8 == float(r)
            nxt[0] = jnp.where(pick, vals1[r], nxt[0])
            nxt[1] = jnp.where(pick, vals1[r + SUBLANES], nxt[1])
            nxt[2] = jnp.where(pick, vals2[r], nxt[2])
            nxt[3] = jnp.where(pick, vals2[r + SUBLANES], nxt[3])
    if u_mrg is None:
        return nxt
    v1a, v1b, v2a, v2b = carry

    hp, ltp, thp, chp, lop = unit_slot(u_mrg)
    lists = [v1a + rows16(v2a, v2b, r) for r in range(PEER_TOPK)]
    hb = v1b + v2a[0:1]
    na = jnp.zeros((SUBLANES, LANES), F32)
    nb = jnp.zeros((SUBLANES, LANES), F32)
    zsum = None
    top = None
    for t in range(PEER_TOPK):
        head = lists[0]
        m = jnp.max(jnp.maximum(head, hb), axis=0, keepdims=True)
        ia = jnp.where(head == m, row8, 99.0)
        ib = jnp.where(hb == m, row8 + float(SUBLANES), 99.0)
        first = jnp.min(jnp.minimum(ia, ib), axis=0, keepdims=True)
        popa = row8 == first
        popb = (row8 + float(SUBLANES)) == first
        if t == 0:
            top = m
            zsum = jnp.ones_like(m)
        else:
            zsum = zsum + jnp.exp(m - top)
        na = na + jnp.where(popa, 1.0, 0.0)
        nb = nb + jnp.where(popb, 1.0, 0.0)
        lists = [jnp.where(popa, lists[k + 1], lists[k]) for k in range(PEER_TOPK - 1 - t)]
        hb = jnp.where(popb, NEG_INF, hb)
    s1p = s_ref[0, hp, ltp]
    s2p = s_ref[1, hp, ltp]
    n1 = jnp.zeros((PEER_NKEYS, LANES), F32)
    for r in range(PEER_TOPK):
        n1 = jnp.where(s1p == rows16(v1a, v1b, r), rows16(na, nb, r), n1)
    n1_ref[thp, chp, :, lop] = n1
    e2_ref[thp, chp, :, lop] = (jnp.exp(s2p - v2a[0:1]) * (1.0 / zsum)).astype(BF16)
    return nxt


def _peer_kernel(h2t_ref, u_ref, vt_ref, s_ref, x1_ref, mod_ref, g2_ref, b2_ref, o_ref,
                 acc_ref, p_ref, act_ref, krow_ref, r2a, e2a, n1a, e1a, r2b, e2b, n1b, e1b, *, tm, te):
    i = pl.program_id(0)
    j = pl.program_id(1)
    nct = tm // TOK_CHUNK
    nlt = tm // LANES
    nblk = te // PEER_NKEYS
    grp_rows = BLK_GROUP * PEER_NKEYS
    zero = jnp.zeros((BF16_ROWS, TOK_CHUNK), BF16)

    def route(tabs):
        carry = None
        for q in range(nlt + 1):
            carry = _route_unit(s_ref, tabs, nlt, q if q < nlt else None, q - 1 if q > 0 else None,
                                carry, tab_head=j)

    def dense(tabs):
        rank2_t, e2_t, n1_t, e1_t = tabs
        k0 = pl.multiple_of(j * nblk, nblk)
        krow_ref[0] = n1_t[:, :, pl.ds(k0, nblk), :]
        krow_ref[1] = e1_t[:, :, pl.ds(k0, nblk), :]

        def act_group(b0):
            rs = slice(b0 * PEER_NKEYS, b0 * PEER_NKEYS + grp_rows)
            act_ref[rs, :] = jnp.dot(u_ref[rs, :], h2t_ref[...], preferred_element_type=F32)

        def key_row(t, hh, c, b):
            halves = [jnp.broadcast_to(krow_ref[t, hh, c, b:b + 1, l0:l0 + LANES],
                                       (BF16_ROWS, LANES)).astype(BF16)
                      for l0 in range(0, TOK_CHUNK, LANES)]
            return jnp.concatenate(halves, axis=1)

        act_group(0)
        for b0 in range(0, nblk, BLK_GROUP):
            blks = range(b0, b0 + BLK_GROUP)
            if b0 + BLK_GROUP < nblk:
                act_group(b0 + BLK_GROUP)
            for c in range(nct):
                cs = slice(c * TOK_CHUNK, (c + 1) * TOK_CHUNK)
                w = {(b, k): zero for b in blks for k in range(NPIECE)}
                for hh in range(PEER_HEADS):
                    n1v = {b: key_row(0, hh, c, b) for b in blks}
                    e1v = {b: key_row(1, hh, c, b) for b in blks}
                    for k in range(NPIECE):
                        ks = slice(k * BF16_ROWS, (k + 1) * BF16_ROWS)
                        r2 = rank2_t[hh, c, ks, :]
                        e2 = e2_t[hh, c, ks, :]
                        for b in blks:
                            w[b, k] = w[b, k] + jnp.where(r2 < n1v[b], e2, zero) * e1v[b]
                for b in blks:
                    rs = slice(b * PEER_NKEYS, (b + 1) * PEER_NKEYS)
                    wb = jnp.concatenate([w[b, k] for k in range(NPIECE)], axis=0)
                    p_ref[rs, cs] = wb * _gelu(act_ref[rs, cs]).astype(BF16)
        acc_ref[...] += jnp.dot(vt_ref[...], p_ref[...], preferred_element_type=F32)

    def step(tabs_cur, tabs_next):
        route(tabs_next)
        dense(tabs_cur)

    tabs_a = (r2a, e2a, n1a, e1a)
    tabs_b = (r2b, e2b, n1b, e1b)

    @pl.when(j == 0)
    def _():
        acc_ref[...] = jnp.zeros_like(acc_ref)

    @pl.when(i == 0)
    def _():
        route(tabs_a)

    @pl.when(jnp.logical_and(i > 0, i % 2 == 1))
    def _():
        step(tabs_a, tabs_b)

    @pl.when(jnp.logical_and(i > 0, i % 2 == 0))
    def _():
        step(tabs_b, tabs_a)

    @pl.when(jnp.logical_and(i > 0, j == pl.num_programs(1) - 1))
    def _():
        mod = mod_ref[0]
        y = acc_ref[...].T
        o_ref[...] = _layer_norm(ALPHA * x1_ref[...] + mod[5:6] * y, g2_ref[...], b2_ref[...])


def _peer_call(h2t, u_b, vt_b, scores, x1f, mod3, ln2_g, ln2_b, S, tm, te):
    D, T = h2t.shape
    ne = u_b.shape[0]
    assert ne // te == PEER_HEADS, "one routing head per expert-tile step"
    nct = tm // TOK_CHUNK
    nlt = tm // LANES
    nblk = te // PEER_NKEYS
    ntile = T // tm
    tiles_per_batch = S // tm
    full = lambda shape: pl.BlockSpec(shape, lambda i, j: (0,) * len(shape))
    prev = lambda i: jnp.maximum(i - 1, 0)
    tab = lambda dt: pltpu.VMEM((PEER_HEADS, nct, PEER_NKEYS, TOK_CHUNK), dt)
    return pl.pallas_call(
        functools.partial(_peer_kernel, tm=tm, te=te),
        grid=(ntile + 1, ne // te),
        in_specs=[pl.BlockSpec((D, tm), lambda i, j: (0, prev(i))),
                  pl.BlockSpec((te, D), lambda i, j: (j, 0)),
                  pl.BlockSpec((D, te), lambda i, j: (0, j)),
                  pl.BlockSpec((2, 1, nlt, PEER_NKEYS, LANES),
                               lambda i, j: (0, j, jnp.minimum(i, ntile - 1), 0, 0)),
                  pl.BlockSpec((tm, D), lambda i, j: (prev(i), 0)),
                  pl.BlockSpec((1, 6, D), lambda i, j: (prev(i) // tiles_per_batch, 0, 0)),
                  full((1, D)), full((1, D))],
        out_specs=pl.BlockSpec((tm, D), lambda i, j: (prev(i), 0)),
        out_shape=jax.ShapeDtypeStruct((T, D), F32),
        scratch_shapes=[pltpu.VMEM((D, tm), F32),
                        pltpu.VMEM((te, tm), BF16),
                        pltpu.VMEM((te, tm), F32),
                        pltpu.VMEM((2, PEER_HEADS, nct, nblk, TOK_CHUNK), F32),
                        tab(BF16), tab(BF16), tab(F32), tab(F32),
                        tab(BF16), tab(BF16), tab(F32), tab(F32)],
        compiler_params=pltpu.CompilerParams(
            dimension_semantics=("arbitrary", "arbitrary"), vmem_limit_bytes=VMEM_LIMIT),
        name="peer",
    )(h2t, u_b, vt_b, scores, x1f, mod3, ln2_g, ln2_b)


def _block_diag(w):
    nh, hd, _ = w.shape
    eye = jnp.eye(nh, dtype=w.dtype)
    return (eye[:, None, :, None] * w[:, :, None, :]).reshape(nh * hd, nh * hd)


def kernel(x, c, ln0_g, ln0_b, ada_w, ada_b, w_in, pool_w, pool_scale, conv_w, conv_b, lru_wa, lru_ba, lru_wx, lru_bx, lru_lambda, w_out, ln1_g, ln1_b, peer_wq, peer_k1, peer_k2, peer_u, peer_v, ln2_g, ln2_b):
    B, S, D = x.shape
    ts, tm, te = SEQ_TILE, TOK_TILE, EXPERT_TILE
    T = B * S
    row = lambda v: v.reshape(1, -1)
    l = 0
    c_pad = jnp.pad(c, ((0, SUBLANES - B), (0, 0)))
    mod = _mod_call(c_pad, ada_w[l], row(ada_b[l]))[:B]
    mod3 = mod.reshape(B, 6, D)

    wg = jnp.concatenate([_block_diag(lru_wa[l]), _block_diag(lru_wx[l])], axis=1).astype(BF16)
    bg = jnp.concatenate([lru_ba[l].reshape(1, -1), lru_bx[l].reshape(1, -1)], axis=1)
    x1, h2t, scores = _front_call(
        x, mod3, row(ln0_g), row(ln0_b), w_in[l].astype(BF16), pool_w[l].astype(BF16),
        row(pool_scale[l]), conv_w[l].reshape(CONV_W, D_LRU), row(conv_b[l]), wg, bg,
        row(lru_lambda[l]), w_out[l].astype(BF16), row(ln1_g[l]), row(ln1_b[l]),
        peer_wq[l].astype(BF16), peer_k1[l].astype(BF16), peer_k2[l].astype(BF16), ts)
    out = _peer_call(h2t, peer_u[l].astype(BF16), peer_v[l].T.astype(BF16), scores,
                     x1.reshape(T, D), mod3, row(ln2_g[l]), row(ln2_b[l]), S, tm, te)
    return out.reshape(B, S, D)
```

```python
import functools

import jax
import jax.numpy as jnp
from jax import lax
from jax.experimental import pallas as pl
from jax.experimental.pallas import tpu as pltpu

F32 = jnp.float32
BF16 = jnp.bfloat16

D_MODEL = 1024
D_POOL = 512
D_LRU = 512
POOL_WINDOWS = (2, 4, 8, 16)
POOL_GW = 128
POOL_HALO = 16
LRU_HEADS = 8
LRU_HD = 64
CONV_W = 4
CONV_HALO = 8
LRU_C = 8.0
PEER_HEADS = 8
PEER_NKEYS = 128
PEER_DKEY = 256
PEER_TOPK = 16
PEER_N = PEER_NKEYS * PEER_NKEYS
LN_EPS = 1e-5
DEPTH = 1
ALPHA = (2.0 * DEPTH) ** 0.25

LANES = 128
SUBLANES = 8
BF16_ROWS = 2 * SUBLANES
NPIECE = PEER_NKEYS // BF16_ROWS
TOK_CHUNK = 2 * LANES
BLK_GROUP = 2
VMEM_LIMIT = 56 * 1024 * 1024
SEQ_TILE = 512
TOK_TILE = 512
EXPERT_TILE = PEER_N // PEER_HEADS
NEG_INF = float("-inf")


def _layer_norm(x, g, b):
    mu = jnp.mean(x, axis=-1, keepdims=True)
    xc = x - mu
    var = jnp.mean(xc * xc, axis=-1, keepdims=True)
    return xc * lax.rsqrt(var + LN_EPS) * g + b


_GELU_A = -2.0 * 0.7978845608028654 * 1.4426950408889634
_GELU_B = _GELU_A * 0.044715


def _gelu(x):
    return x / (1.0 + jnp.exp2(x * (_GELU_A + _GELU_B * (x * x))))


def _sigmoid(x):
    return 1.0 / (1.0 + jnp.exp(-x))


def _mod_kernel(c_ref, w_ref, b_ref, o_ref):
    c = c_ref[...]
    cond = c * _sigmoid(c)
    o_ref[...] = jnp.dot(cond, w_ref[...], preferred_element_type=F32) + b_ref[...]


def _mod_call(c_pad, ada_w, ada_b, tn=1024):
    rows, d = c_pad.shape
    n = ada_w.shape[1]
    return pl.pallas_call(
        _mod_kernel,
        grid=(n // tn,),
        in_specs=[pl.BlockSpec((rows, d), lambda j: (0, 0)),
                  pl.BlockSpec((d, tn), lambda j: (0, j)),
                  pl.BlockSpec((1, tn), lambda j: (0, j))],
        out_specs=pl.BlockSpec((rows, tn), lambda j: (0, j)),
        out_shape=jax.ShapeDtypeStruct((rows, n), F32),
        name="mod",
    )(c_pad, ada_w, ada_b)


def _front_kernel(x_ref, modm_ref, modr_ref, g0_ref, b0_ref, win_ref, pw_ref, ps_ref, cw_ref, cb_ref,
                  wg_ref, bg_ref, lam_ref, wout_ref, g1_ref, b1_ref, wq_ref, k1_ref, k2_ref, v_ref,
                  x1_ref, h2t_ref, s_ref, vt_ref,
                  zp_ext, zx_ext, a_s, b_s, h_s, hc_ref, x0_prev, ym_prev, *, ts, nseq):
    k = pl.program_id(0)
    s = jnp.minimum(k, pl.num_programs(0) - 2) % nseq

    @pl.when(k == 0)
    def _():
        x0_prev[...] = jnp.zeros_like(x0_prev)
        ym_prev[...] = jnp.zeros_like(ym_prev)

    @pl.when(s == 0)
    def _():
        zp_ext[0:POOL_HALO, :] = jnp.zeros((POOL_HALO, D_POOL), F32)
        zx_ext[0:CONV_HALO, :] = jnp.zeros((CONV_HALO, D_LRU), F32)
        hc_ref[...] = jnp.zeros_like(hc_ref)

    @pl.when(s > 0)
    def _():
        zp_ext[0:POOL_HALO, :] = zp_ext[ts:ts + POOL_HALO, :]
        zx_ext[0:CONV_HALO, :] = zx_ext[ts:ts + CONV_HALO, :]

    nlt = ts // LANES
    modr = modr_ref[0]
    y = jnp.dot(ym_prev[...], wout_ref[...], preferred_element_type=F32)
    x1 = _layer_norm(ALPHA * x0_prev[...] + modr[2:3] * y, g1_ref[...], b1_ref[...])
    x1_ref[0] = x1
    h2 = x1 * (1.0 + modr[4:5]) + modr[3:4]
    h2t_ref[...] = h2.T.astype(BF16)
    q = jnp.dot(h2.astype(BF16), wq_ref[...], preferred_element_type=F32)
    half = PEER_DKEY // 2
    nt = (((1,), (1,)), ((), ()))
    for hh in range(PEER_HEADS):
        qh = q[:, hh * PEER_DKEY:(hh + 1) * PEER_DKEY]
        mu = jnp.mean(qh, axis=-1, keepdims=True)
        qc = qh - mu
        var = jnp.mean(qc * qc, axis=-1, keepdims=True)
        qb = (qc * lax.rsqrt(var + LN_EPS)).astype(BF16)
        s1t = lax.dot_general(k1_ref[...], qb[:, :half], nt, preferred_element_type=F32)
        s2t = lax.dot_general(k2_ref[...], qb[:, half:], nt, preferred_element_type=F32)
        for lt in range(nlt):
            s_ref[0, hh, lt] = s1t[:, lt * LANES:(lt + 1) * LANES]
            s_ref[1, hh, lt] = s2t[:, lt * LANES:(lt + 1) * LANES]

    vt_ref[...] = v_ref[...].T.astype(BF16)

    x0 = _layer_norm(x_ref[0], g0_ref[...], b0_ref[...])
    mod = modm_ref[0]
    h = x0 * (1.0 + mod[1:2]) + mod[0:1]
    z = jnp.dot(h.astype(BF16), win_ref[...], preferred_element_type=F32)
    zp_ext[POOL_HALO:, :] = z[:, :D_POOL]
    zx_ext[CONV_HALO:, :] = z[:, D_POOL:D_POOL + D_LRU]
    zg = z[:, D_POOL + D_LRU:]

    t_glob = lax.broadcasted_iota(jnp.int32, (ts, POOL_GW), 0) + s * ts
    outs = []
    for g, w in enumerate(POOL_WINDOWS):
        cs = slice(g * POOL_GW, (g + 1) * POOL_GW)
        cur = zp_ext[POOL_HALO:POOL_HALO + ts, cs]
        acc = cur
        for k in range(1, w):
            acc = acc + zp_ext[POOL_HALO - k:POOL_HALO - k + ts, cs]
        cnt = jnp.minimum(t_glob + 1, w).astype(F32)
        y = acc / cnt - cur
        outs.append(jnp.dot(y.astype(BF16), pw_ref[g], preferred_element_type=F32))
    y_pool = jnp.concatenate(outs, axis=1) * ps_ref[...]

    cw = cw_ref[...]
    xc = cb_ref[...] + cw[0:1] * zx_ext[CONV_HALO - 3:CONV_HALO - 3 + ts, :]
    for k in range(1, CONV_W):
        xc = xc + cw[k:k + 1] * zx_ext[CONV_HALO - 3 + k:CONV_HALO - 3 + k + ts, :]
    gates = jnp.dot(xc.astype(BF16), wg_ref[...], preferred_element_type=F32) + bg_ref[...]
    r = _sigmoid(gates[:, :D_LRU])
    i = _sigmoid(gates[:, D_LRU:])
    nl = -lam_ref[...]
    softplus = jnp.maximum(nl, 0.0) + jnp.log(1.0 + jnp.exp(-jnp.abs(nl)))
    a = jnp.exp((-LRU_C * softplus) * r)
    om = 1.0 - a * a
    bv = jnp.where(om > 0.0, om * lax.rsqrt(om), 0.0) * (i * xc)

    a = a.reshape(ts // SUBLANES, SUBLANES, D_LRU)
    bv = bv.reshape(ts // SUBLANES, SUBLANES, D_LRU)
    row_in_grp = lax.broadcasted_iota(jnp.int32, a.shape, 1)
    for k in (1, 2, 4):
        a_sh = pltpu.roll(a, k, 1)
        b_sh = pltpu.roll(bv, k, 1)
        valid = row_in_grp >= k
        bv = jnp.where(valid, a * b_sh + bv, bv)
        a = jnp.where(valid, a * a_sh, a)
    a_s[...] = a.reshape(ts, D_LRU)
    b_s[...] = bv.reshape(ts, D_LRU)

    def grp(gi, hprev):
        off = pl.multiple_of(gi * SUBLANES, SUBLANES)
        hg = a_s[pl.ds(off, SUBLANES), :] * hprev + b_s[pl.ds(off, SUBLANES), :]
        h_s[pl.ds(off, SUBLANES), :] = hg
        return hg[SUBLANES - 1:SUBLANES, :]

    hc_ref[0:1, :] = lax.fori_loop(0, ts // SUBLANES, grp, hc_ref[0:1, :])
    y_lru = h_s[...] * _gelu(zg)
    x0_prev[...] = x0
    ym_prev[...] = jnp.concatenate([y_pool, y_lru], axis=1).astype(BF16)


def _front_call(x, mod3, ln0_g, ln0_b, w_in, pool_w, pool_scale, conv_w, conv_b, wg, bg, lam,
                w_out, ln1_g, ln1_b, w_q, k1, k2, v, ts):
    B, S, D = x.shape
    T = B * S
    nseq = S // ts
    ntile = B * nseq
    ne = v.shape[0]
    vrows = ne // ntile
    nlt = ts // LANES
    dz = w_in.shape[1]
    dq = w_q.shape[1]
    full = lambda shape: pl.BlockSpec(shape, lambda k: (0,) * len(shape))
    cur = lambda k: jnp.minimum(k, ntile - 1)
    prev = lambda k: jnp.maximum(k - 1, 0)
    return pl.pallas_call(
        functools.partial(_front_kernel, ts=ts, nseq=nseq),
        grid=(ntile + 1,),
        in_specs=[pl.BlockSpec((1, ts, D), lambda k: (cur(k) // nseq, cur(k) % nseq, 0)),
                  pl.BlockSpec((1, 6, D), lambda k: (cur(k) // nseq, 0, 0)),
                  pl.BlockSpec((1, 6, D), lambda k: (prev(k) // nseq, 0, 0)),
                  full((1, D)), full((1, D)), full((D, dz)),
                  full((len(POOL_WINDOWS), POOL_GW, POOL_GW)), full((1, D_POOL)),
                  full((CONV_W, D_LRU)), full((1, D_LRU)),
                  full((D_LRU, 2 * D_LRU)), full((1, 2 * D_LRU)), full((1, D_LRU)),
                  full((D, D)), full((1, D)), full((1, D)), full((D, dq)),
                  full((PEER_NKEYS, PEER_DKEY // 2)), full((PEER_NKEYS, PEER_DKEY // 2)),
                  pl.BlockSpec((vrows, D), lambda k: (cur(k), 0))],
        out_specs=[pl.BlockSpec((1, ts, D), lambda k: (prev(k) // nseq, prev(k) % nseq, 0)),
                   pl.BlockSpec((D, ts), lambda k: (0, prev(k))),
                   pl.BlockSpec((2, PEER_HEADS, nlt, PEER_NKEYS, LANES), lambda k: (0, 0, prev(k), 0, 0)),
                   pl.BlockSpec((D, vrows), lambda k: (0, cur(k)))],
        out_shape=[jax.ShapeDtypeStruct((B, S, D), F32),
                   jax.ShapeDtypeStruct((D, T), BF16),
                   jax.ShapeDtypeStruct((2, PEER_HEADS, T // LANES, PEER_NKEYS, LANES), F32),
                   jax.ShapeDtypeStruct((D, ne), BF16)],
        scratch_shapes=[pltpu.VMEM((POOL_HALO + ts, D_POOL), F32),
                        pltpu.VMEM((CONV_HALO + ts, D_LRU), F32),
                        pltpu.VMEM((ts, D_LRU), F32),
                        pltpu.VMEM((ts, D_LRU), F32),
                        pltpu.VMEM((ts, D_LRU), F32),
                        pltpu.VMEM((SUBLANES, D_LRU), F32),
                        pltpu.VMEM((ts, D), F32),
                        pltpu.VMEM((ts, D), BF16)],
        compiler_params=pltpu.CompilerParams(
            dimension_semantics=("arbitrary",), vmem_limit_bytes=VMEM_LIMIT),
        name="front",
    )(x, mod3, mod3, ln0_g, ln0_b, w_in, pool_w, pool_scale, conv_w, conv_b, wg, bg, lam,
      w_out, ln1_g, ln1_b, w_q, k1, k2, v)


def _oddeven_merge_sort_pairs(n):
    pairs = []
    p = 1
    while p < n:
        k = p
        while k >= 1:
            for j in range(k % p, n - k, 2 * k):
                for i in range(min(k, n - j - k)):
                    if (i + j) // (2 * p) == (i + j + k) // (2 * p):
                        pairs.append((i + j, i + j + k))
            k //= 2
        p *= 2
    return pairs


_SORT16 = _oddeven_merge_sort_pairs(PEER_TOPK)


def _top16_sorted(rows):
    v = list(rows)
    for i, j in _SORT16:
        v[i], v[j] = jnp.maximum(v[i], v[j]), jnp.minimum(v[i], v[j])
    shift = SUBLANES // 2
    while shift >= 1:
        other = [pltpu.roll(x, shift, 0) for x in v]
        v = [jnp.maximum(v[k], other[PEER_TOPK - 1 - k]) for k in range(PEER_TOPK)]
        d = PEER_TOPK // 2
        while d >= 1:
            for i in range(PEER_TOPK):
                if i & d == 0:
                    v[i], v[i + d] = jnp.maximum(v[i], v[i + d]), jnp.minimum(v[i], v[i + d])
            d //= 2
        shift //= 2
    return v


def _route_unit(s_ref, tabs, nlt, u_ext, u_mrg, carry, tab_head=None):
    rank2_ref, e2_ref, n1_ref, e1_ref = tabs
    row8 = lax.broadcasted_iota(jnp.int32, (SUBLANES, LANES), 0).astype(F32)
    per_chunk = TOK_CHUNK // LANES

    def unit_slot(u):
        hh = u // nlt
        lt = u % nlt
        if isinstance(u, int):
            lo = slice((lt % per_chunk) * LANES, (lt % per_chunk + 1) * LANES)
        else:
            lo = pl.ds(pl.multiple_of((lt % per_chunk) * LANES, LANES), LANES)
        return hh, lt, (hh if tab_head is None else tab_head), lt // per_chunk, lo

    def rows16(va, vb, r):
        return va[r:r + 1] if r < SUBLANES else vb[r - SUBLANES:r - SUBLANES + 1]

    nxt = None
    if u_ext is not None:
        hh, lt, th, ch, lo = unit_slot(u_ext)
        s1 = s_ref[0, hh, lt]
        s2 = s_ref[1, hh, lt]
        nrow = PEER_NKEYS // SUBLANES
        rows2 = [s2[k * SUBLANES:(k + 1) * SUBLANES] for k in range(nrow)]
        vals1 = _top16_sorted([s1[k * SUBLANES:(k + 1) * SUBLANES] for k in range(nrow)])
        vals2 = _top16_sorted(rows2)
        rank2 = []
        for row in rows2:
            cnt = jnp.where(vals2[0] > row, 1.0, 0.0)
            for r in range(1, PEER_TOPK):
                cnt = cnt + jnp.where(vals2[r] > row, 1.0, 0.0)
            rank2.append(cnt)
        rank2_ref[th, ch, :, lo] = jnp.concatenate(rank2, axis=0).astype(BF16)
        e1_ref[th, ch, :, lo] = jnp.exp(s1 - vals1[0][0:1])
        nxt = [jnp.zeros((SUBLANES, LANES), F32) for _ in range(4)]
        for r in range(SUBLANES):
            pick = row8 == float(r)
            nxt[0] = jnp.where(pick, vals1[r], nxt[0])
            nxt[1] = jnp.where(pick, vals1[r + SUBLANES], nxt[1])
            nxt[2] = jnp.where(pick, vals2[r], nxt[2])
            nxt[3] = jnp.where(pick, vals2[r + SUBLANES], nxt[3])
    if u_mrg is None:
        return nxt
    v1a, v1b, v2a, v2b = carry

    hp, ltp, thp, chp, lop = unit_slot(u_mrg)
    lists = [v1a + rows16(v2a, v2b, r) for r in range(PEER_TOPK)]
    hb = v1b + v2a[0:1]
    na = jnp.zeros((SUBLANES, LANES), F32)
    nb = jnp.zeros((SUBLANES, LANES), F32)
    zsum = None
    top = None
    for t in range(PEER_TOPK):
        head = lists[0]
        m = jnp.max(jnp.maximum(head, hb), axis=0, keepdims=True)
        ia = jnp.where(head == m, row8, 99.0)
        ib = jnp.where(hb == m, row8 + float(SUBLANES), 99.0)
        first = jnp.min(jnp.minimum(ia, ib), axis=0, keepdims=True)
        popa = row8 == first
        popb = (row8 + float(SUBLANES)) == first
        if t == 0:
            top = m
            zsum = jnp.ones_like(m)
        else:
            zsum = zsum + jnp.exp(m - top)
        na = na + jnp.where(popa, 1.0, 0.0)
        nb = nb + jnp.where(popb, 1.0, 0.0)
        lists = [jnp.where(popa, lists[k + 1], lists[k]) for k in range(PEER_TOPK - 1 - t)]
        hb = jnp.where(popb, NEG_INF, hb)
    s1p = s_ref[0, hp, ltp]
    s2p = s_ref[1, hp, ltp]
    n1 = jnp.zeros((PEER_NKEYS, LANES), F32)
    for r in range(PEER_TOPK):
        n1 = jnp.where(s1p == rows16(v1a, v1b, r), rows16(na, nb, r), n1)
    n1_ref[thp, chp, :, lop] = n1
    e2_ref[thp, chp, :, lop] = (jnp.exp(s2p - v2a[0:1]) * (1.0 / zsum)).astype(BF16)
    return nxt


def _peer_kernel(h2t_ref, u_ref, vt_ref, s_ref, x1_ref, mod_ref, g2_ref, b2_ref, o_ref,
                 acc_ref, p_ref, act_ref, krow_ref, r2a, e2a, n1a, e1a, r2b, e2b, n1b, e1b, *, tm, te):
    i = pl.program_id(0)
    j = pl.program_id(1)
    nct = tm // TOK_CHUNK
    nlt = tm // LANES
    nblk = te // PEER_NKEYS
    grp_rows = BLK_GROUP * PEER_NKEYS
    zero = jnp.zeros((BF16_ROWS, TOK_CHUNK), BF16)

    def route(tabs):
        carry = None
        for q in range(nlt + 1):
            carry = _route_unit(s_ref, tabs, nlt, q if q < nlt else None, q - 1 if q > 0 else None,
                                carry, tab_head=j)

    def dense(tabs):
        rank2_t, e2_t, n1_t, e1_t = tabs
        k0 = pl.multiple_of(j * nblk, nblk)
        krow_ref[0] = n1_t[:, :, pl.ds(k0, nblk), :]
        krow_ref[1] = e1_t[:, :, pl.ds(k0, nblk), :]

        def act_group(b0):
            rs = slice(b0 * PEER_NKEYS, b0 * PEER_NKEYS + grp_rows)
            act_ref[rs, :] = jnp.dot(u_ref[rs, :], h2t_ref[...], preferred_element_type=F32)

        def key_row(t, hh, c, b):
            halves = [jnp.broadcast_to(krow_ref[t, hh, c, b:b + 1, l0:l0 + LANES],
                                       (BF16_ROWS, LANES)).astype(BF16)
                      for l0 in range(0, TOK_CHUNK, LANES)]
            return jnp.concatenate(halves, axis=1)

        act_group(0)
        for b0 in range(0, nblk, BLK_GROUP):
            blks = range(b0, b0 + BLK_GROUP)
            if b0 + BLK_GROUP < nblk:
                act_group(b0 + BLK_GROUP)
            for c in range(nct):
                cs = slice(c * TOK_CHUNK, (c + 1) * TOK_CHUNK)
                w = {(b, k): zero for b in blks for k in range(NPIECE)}
                for hh in range(PEER_HEADS):
                    n1v = {b: key_row(0, hh, c, b) for b in blks}
                    e1v = {b: key_row(1, hh, c, b) for b in blks}
                    for k in range(NPIECE):
                        ks = slice(k * BF16_ROWS, (k + 1) * BF16_ROWS)
                        r2 = rank2_t[hh, c, ks, :]
                        e2 = e2_t[hh, c, ks, :]
                        for b in blks:
                            w[b, k] = w[b, k] + jnp.where(r2 < n1v[b], e2, zero) * e1v[b]
                for b in blks:
                    rs = slice(b * PEER_NKEYS, (b + 1) * PEER_NKEYS)
                    wb = jnp.concatenate([w[b, k] for k in range(NPIECE)], axis=0)
                    p_ref[rs, cs] = wb * _gelu(act_ref[rs, cs]).astype(BF16)
        acc_ref[...] += jnp.dot(vt_ref[...], p_ref[...], preferred_element_type=F32)

    def step(tabs_cur, tabs_next):
        @pl.when(j == 0)
        def _():
            acc_ref[...] = jnp.zeros_like(acc_ref)

        route(tabs_next)
        dense(tabs_cur)

        @pl.when(j == pl.num_programs(1) - 1)
        def _():
            mod = mod_ref[0]
            y = acc_ref[...].T
            o_ref[...] = _layer_norm(ALPHA * x1_ref[...] + mod[5:6] * y, g2_ref[...], b2_ref[...])

    tabs_a = (r2a, e2a, n1a, e1a)
    tabs_b = (r2b, e2b, n1b, e1b)

    @pl.when(i == 0)
    def _():
        route(tabs_a)

    @pl.when(jnp.logical_and(i > 0, i % 2 == 1))
    def _():
        step(tabs_a, tabs_b)

    @pl.when(jnp.logical_and(i > 0, i % 2 == 0))
    def _():
        step(tabs_b, tabs_a)


def _peer_call(h2t, u_b, vt_b, scores, x1f, mod3, ln2_g, ln2_b, S, tm, te):
    D, T = h2t.shape
    ne = u_b.shape[0]
    assert ne // te == PEER_HEADS, "one routing head per expert-tile step"
    nct = tm // TOK_CHUNK
    nlt = tm // LANES
    nblk = te // PEER_NKEYS
    ntile = T // tm
    tiles_per_batch = S // tm
    full = lambda shape: pl.BlockSpec(shape, lambda i, j: (0,) * len(shape))
    prev = lambda i: jnp.maximum(i - 1, 0)
    tab = lambda dt: pltpu.VMEM((PEER_HEADS, nct, PEER_NKEYS, TOK_CHUNK), dt)
    return pl.pallas_call(
        functools.partial(_peer_kernel, tm=tm, te=te),
        grid=(ntile + 1, ne // te),
        in_specs=[pl.BlockSpec((D, tm), lambda i, j: (0, prev(i))),
                  pl.BlockSpec((te, D), lambda i, j: (j, 0)),
                  pl.BlockSpec((D, te), lambda i, j: (0, j)),
                  pl.BlockSpec((2, 1, nlt, PEER_NKEYS, LANES),
                               lambda i, j: (0, j, jnp.minimum(i, ntile - 1), 0, 0)),
                  pl.BlockSpec((tm, D), lambda i, j: (prev(i), 0)),
                  pl.BlockSpec((1, 6, D), lambda i, j: (prev(i) // tiles_per_batch, 0, 0)),
                  full((1, D)), full((1, D))],
        out_specs=pl.BlockSpec((tm, D), lambda i, j: (prev(i), 0)),
        out_shape=jax.ShapeDtypeStruct((T, D), F32),
        scratch_shapes=[pltpu.VMEM((D, tm), F32),
                        pltpu.VMEM((te, tm), BF16),
                        pltpu.VMEM((te, tm), F32),
                        pltpu.VMEM((2, PEER_HEADS, nct, nblk, TOK_CHUNK), F32),
                        tab(BF16), tab(BF16), tab(F32), tab(F32),
                        tab(BF16), tab(BF16), tab(F32), tab(F32)],
        compiler_params=pltpu.CompilerParams(
            dimension_semantics=("arbitrary", "arbitrary"), vmem_limit_bytes=VMEM_LIMIT),
        name="peer",
    )(h2t, u_b, vt_b, scores, x1f, mod3, ln2_g, ln2_b)


def _block_diag(w):
    nh, hd, _ = w.shape
    eye = jnp.eye(nh, dtype=w.dtype)
    return (eye[:, None, :, None] * w[:, :, None, :]).reshape(nh * hd, nh * hd)


def kernel(x, c, ln0_g, ln0_b, ada_w, ada_b, w_in, pool_w, pool_scale, conv_w, conv_b, lru_wa, lru_ba, lru_wx, lru_bx, lru_lambda, w_out, ln1_g, ln1_b, peer_wq, peer_k1, peer_k2, peer_u, peer_v, ln2_g, ln2_b):
    B, S, D = x.shape
    ts, tm, te = SEQ_TILE, TOK_TILE, EXPERT_TILE
    T = B * S
    row = lambda v: v.reshape(1, -1)
    l = 0
    c_pad = jnp.pad(c, ((0, SUBLANES - B), (0, 0)))
    mod = _mod_call(c_pad, ada_w[l], row(ada_b[l]))[:B]
    mod3 = mod.reshape(B, 6, D)

    wg = jnp.concatenate([_block_diag(lru_wa[l]), _block_diag(lru_wx[l])], axis=1).astype(BF16)
    bg = jnp.concatenate([lru_ba[l].reshape(1, -1), lru_bx[l].reshape(1, -1)], axis=1)
    x1, h2t, scores, vt_b = _front_call(
        x, mod3, row(ln0_g), row(ln0_b), w_in[l].astype(BF16), pool_w[l].astype(BF16),
        row(pool_scale[l]), conv_w[l].reshape(CONV_W, D_LRU), row(conv_b[l]), wg, bg,
        row(lru_lambda[l]), w_out[l].astype(BF16), row(ln1_g[l]), row(ln1_b[l]),
        peer_wq[l].astype(BF16), peer_k1[l].astype(BF16), peer_k2[l].astype(BF16), peer_v[l], ts)
    out = _peer_call(h2t, peer_u[l].astype(BF16), vt_b, scores,
                     x1.reshape(T, D), mod3, row(ln2_g[l]), row(ln2_b[l]), S, tm, te)
    return out.reshape(B, S, D)
```
